```python
import jax
import jax.numpy as jnp
from jax import lax
import numpy as np

D_MODEL = 4096
BATCH = 4
SEQ = 4096
DEPTH = 2

HEAD_DIM = 128
ROT_DIM = HEAD_DIM // 4
ROPE_THETA = 500000.0
NORM_EPS = 1e-6
MASK_VALUE = -1e30

DIL_GROUPS = ((128, 1), (512, 4), (2048, 16))
DIL_HEADS_PER_GROUP = 4
DIL_HEADS = len(DIL_GROUPS) * DIL_HEADS_PER_GROUP
DIL_WIDTH = DIL_HEADS * HEAD_DIM
DIL_OUT = DIL_HEADS_PER_GROUP * HEAD_DIM
MLA_HEADS = 8
MLA_Q_LORA = 1536
MLA_KV_LORA = 512
MLA_NOPE = 128
MLA_ROPE = 64
MLA_V = 128
MLA_QBLOCK = 128
MOBA_HEADS = 8
MOBA_BLOCK = 256
MOBA_TOPK = 3
MOBA_QCHUNK = 32
DSA_HEADS = 8
DSA_TOPK = 256
IDX_HEADS = 32
IDX_DIM = 64
IDX_ROT = IDX_DIM // 4
DSA_QCHUNK = 128

N_BRANCH = 4
IN_SIZES = (DIL_WIDTH, DIL_WIDTH, DIL_WIDTH,
            MLA_Q_LORA, MLA_KV_LORA, MLA_ROPE,
            MOBA_HEADS * HEAD_DIM, MOBA_HEADS * HEAD_DIM, MOBA_HEADS * HEAD_DIM,
            DSA_HEADS * HEAD_DIM, HEAD_DIM, HEAD_DIM, IDX_HEADS * IDX_DIM, IDX_DIM, IDX_HEADS,
            N_BRANCH * D_MODEL)
D_IN = sum(IN_SIZES)
IN_SPLITS = tuple(sum(IN_SIZES[:i + 1]) for i in range(len(IN_SIZES) - 1))
BRANCH_SIZES = (DIL_OUT, MLA_HEADS * MLA_V, MOBA_HEADS * HEAD_DIM, DSA_HEADS * HEAD_DIM)
D_BRANCH = sum(BRANCH_SIZES)
BRANCH_SPLITS = tuple(sum(BRANCH_SIZES[:i + 1]) for i in range(len(BRANCH_SIZES) - 1))

D_FF_DENSE = 14336
N_EXPERTS = 8
MOE_TOP_K = 2
D_FF_EXPERT = 3072
MOE_ROWS = 256
N_MOD = 6

kernel_name = 'hybrid_gated_sparse_attn_moe_decoder'


def rms_norm(x, gain):
    xf = x.astype(jnp.float32)
    y = xf * lax.rsqrt(jnp.mean(xf * xf, axis=-1, keepdims=True) + NORM_EPS)
    return (y * gain.astype(jnp.float32)).astype(x.dtype)


def rotary(x, positions, rot_dim):
    half = rot_dim // 2
    freqs = ROPE_THETA ** (-jnp.arange(half, dtype=jnp.float32) * (2.0 / rot_dim))
    ang = positions.astype(jnp.float32)[..., None] * freqs
    ang = ang.reshape(ang.shape[:2] + (1,) * (x.ndim - 3) + (half,))
    cos, sin = jnp.cos(ang), jnp.sin(ang)
    x1 = x[..., :half].astype(jnp.float32)
    x2 = x[..., half:rot_dim].astype(jnp.float32)
    rot = jnp.concatenate([x1 * cos - x2 * sin, x2 * cos + x1 * sin], axis=-1).astype(x.dtype)
    return jnp.concatenate([rot, x[..., rot_dim:]], axis=-1)


def dilated_window_attention(q, k, v, window, dilation):
    B, S, Hg, hd = q.shape
    n = S // dilation
    wsub = window // dilation
    nb = -(-n // wsub)
    npad = nb * wsub

    def to_blocks(t):
        t = t.reshape(B, n, dilation, Hg, hd).transpose(0, 2, 3, 1, 4)
        t = jnp.pad(t, ((0, 0), (0, 0), (0, 0), (0, npad - n), (0, 0)))
        return t.reshape(B, dilation, Hg, nb, wsub, hd)

    def with_prev(t):
        prev = jnp.pad(t[:, :, :, :-1], ((0, 0), (0, 0), (0, 0), (1, 0), (0, 0), (0, 0)))
        return jnp.concatenate([prev, t], axis=4)

    qb = to_blocks(q)
    kw = with_prev(to_blocks(k))
    vw = with_prev(to_blocks(v))
    s = jnp.einsum('brhnqd,brhnkd->brhnqk', qb, kw, preferred_element_type=jnp.float32) * (hd ** -0.5)
    ki = jnp.arange(2 * wsub)[None, :]
    dist = (jnp.arange(wsub)[:, None] + wsub) - ki
    band = (dist >= 0) & (dist <= wsub)
    has_prev = (jnp.arange(nb) > 0)[:, None, None] | (ki >= wsub)[None]
    s = jnp.where(band[None] & has_prev, s, MASK_VALUE)
    lse = jax.nn.logsumexp(s, axis=-1)
    p = jnp.exp(s - lse[..., None])
    o = jnp.einsum('brhnqk,brhnkd->brhnqd', p.astype(v.dtype), vw)
    o = o.reshape(B, dilation, Hg, npad, hd)[:, :, :, :n].transpose(0, 3, 1, 2, 4).reshape(B, S, Hg, hd)
    lse = lse.reshape(B, dilation, Hg, npad)[..., :n].transpose(0, 3, 1, 2).reshape(B, S, Hg)
    return o, lse


def dilated_mixture(q, k, v, positions):
    B, S, _ = q.shape
    q = rotary(q.reshape(B, S, DIL_HEADS, HEAD_DIM), positions, ROT_DIM)
    k = rotary(k.reshape(B, S, DIL_HEADS, HEAD_DIM), positions, ROT_DIM)
    v = v.reshape(B, S, DIL_HEADS, HEAD_DIM)
    outs, lses = [], []
    for g, (window, dilation) in enumerate(DIL_GROUPS):
        hs = slice(g * DIL_HEADS_PER_GROUP, (g + 1) * DIL_HEADS_PER_GROUP)
        o_g, lse_g = dilated_window_attention(q[:, :, hs], k[:, :, hs], v[:, :, hs], window, dilation)
        outs.append(o_g)
        lses.append(lse_g)
    w = jax.nn.softmax(jnp.stack(lses), axis=0)
    o = jnp.einsum('gbsh,gbshd->bshd', w, jnp.stack(outs).astype(jnp.float32))
    return o.astype(q.dtype).reshape(B, S, DIL_OUT)


def mla_attention(cq, ckv, k_rope_in, positions, q_norm, w_q_up, kv_norm, w_kv_up):
    B, S, _ = cq.shape
    H = MLA_HEADS
    q = (rms_norm(cq, q_norm) @ w_q_up).reshape(B, S, H, MLA_NOPE + MLA_ROPE)
    q_nope = q[..., :MLA_NOPE]
    q_pe = rotary(q[..., MLA_NOPE:], positions, MLA_ROPE)
    kv = (rms_norm(ckv, kv_norm) @ w_kv_up).reshape(B, S, H, MLA_NOPE + MLA_V)
    k_nope, v = kv[..., :MLA_NOPE], kv[..., MLA_NOPE:]
    k_pe = rotary(k_rope_in, positions, MLA_ROPE)
    nq = S // MLA_QBLOCK
    scale = (MLA_NOPE + MLA_ROPE) ** -0.5
    kpos = jnp.arange(S)

    def block(args):
        qn, qp, i = args
        qpos = i * MLA_QBLOCK + jnp.arange(MLA_QBLOCK)
        s = (jnp.einsum('bqhd,bkhd->bhqk', qn, k_nope, preferred_element_type=jnp.float32)
             + jnp.einsum('bqhr,bkr->bhqk', qp, k_pe, preferred_element_type=jnp.float32)) * scale
        s = jnp.where(kpos[None, :] <= qpos[:, None], s, MASK_VALUE)
        p = jax.nn.softmax(s, axis=-1)
        return jnp.einsum('bhqk,bkhd->bqhd', p.astype(v.dtype), v)

    def to_blocks(t):
        return t.reshape((B, nq, MLA_QBLOCK) + t.shape[2:]).swapaxes(0, 1)

    o = lax.map(block, (to_blocks(q_nope), to_blocks(q_pe), jnp.arange(nq)))
    return o.swapaxes(0, 1).reshape(B, S, H * MLA_V)


def moba_attention(q, k, v, positions):
    B, S, _ = q.shape
    H, hd = MOBA_HEADS, HEAD_DIM
    q = rotary(q.reshape(B, S, H, hd), positions, ROT_DIM)
    k = rotary(k.reshape(B, S, H, hd), positions, ROT_DIM)
    v = v.reshape(B, S, H, hd)
    nb = -(-S // MOBA_BLOCK)
    sp = nb * MOBA_BLOCK
    pad = ((0, 0), (0, sp - S), (0, 0), (0, 0))
    q, k, v = jnp.pad(q, pad), jnp.pad(k, pad), jnp.pad(v, pad)
    kb = k.reshape(B, nb, MOBA_BLOCK, H, hd).transpose(0, 3, 1, 2, 4)
    vb = v.reshape(B, nb, MOBA_BLOCK, H, hd).transpose(0, 3, 1, 2, 4)
    k_mean = jnp.mean(kb.astype(jnp.float32), axis=3)
    n_sel = min(MOBA_TOPK, nb)
    scale = hd ** -0.5
    bi = jnp.arange(B)[:, None, None, None]
    hi = jnp.arange(H)[None, :, None, None]
    blk_ids = jnp.arange(nb)

    def chunk(args):
        qc, ci = args
        q0 = ci * MOBA_QCHUNK
        qpos = q0 + jnp.arange(MOBA_QCHUNK)
        own = q0 // MOBA_BLOCK
        gate = jnp.einsum('bqhd,bhnd->bhqn', qc.astype(jnp.float32), k_mean)
        gate = jnp.where(blk_ids < own, gate, MASK_VALUE)
        _, sel = lax.top_k(gate, n_sel)
        sel_ok = sel < own
        k_sel = kb[bi, hi, sel]
        v_sel = vb[bi, hi, sel]
        s_sel = jnp.einsum('bqhd,bhqjkd->bhqjk', qc, k_sel, preferred_element_type=jnp.float32) * scale
        s_sel = jnp.where(sel_ok[..., None], s_sel, MASK_VALUE).reshape(B, H, MOBA_QCHUNK, n_sel * MOBA_BLOCK)
        k_own = lax.dynamic_index_in_dim(kb, own, axis=2, keepdims=False)
        v_own = lax.dynamic_index_in_dim(vb, own, axis=2, keepdims=False)
        kpos = own * MOBA_BLOCK + jnp.arange(MOBA_BLOCK)
        s_own = jnp.einsum('bqhd,bhkd->bhqk', qc, k_own, preferred_element_type=jnp.float32) * scale
        s_own = jnp.where(kpos[None, :] <= qpos[:, None], s_own, MASK_VALUE)
        p = jax.nn.softmax(jnp.concatenate([s_sel, s_own], axis=-1), axis=-1)
        p_sel = p[..., :n_sel * MOBA_BLOCK].reshape(B, H, MOBA_QCHUNK, n_sel, MOBA_BLOCK).astype(v.dtype)
        p_own = p[..., n_sel * MOBA_BLOCK:].astype(v.dtype)
        return (jnp.einsum('bhqjk,bhqjkd->bqhd', p_sel, v_sel)
                + jnp.einsum('bhqk,bhkd->bqhd', p_own, v_own))

    nch = sp // MOBA_QCHUNK
    qch = q.reshape(B, nch, MOBA_QCHUNK, H, hd).swapaxes(0, 1)
    o = lax.map(chunk, (qch, jnp.arange(nch)))
    return o.swapaxes(0, 1).reshape(B, sp, H * hd)[:, :S]


def dsa_attention(q, k, v, iq, ik, iw, positions):
    B, S, _ = q.shape
    H, hd = DSA_HEADS, HEAD_DIM
    q = rotary(q.reshape(B, S, H, hd), positions, ROT_DIM)
    k = rotary(k, positions, ROT_DIM)
    iq = rotary(iq.reshape(B, S, IDX_HEADS, IDX_DIM), positions, IDX_ROT)
    ik = rotary(ik, positions, IDX_ROT)
    iw = iw.astype(jnp.float32) * (IDX_HEADS ** -0.5)
    n_keep = min(DSA_TOPK, S // 4)
    nch = S // DSA_QCHUNK
    kpos = jnp.arange(S)
    scale = hd ** -0.5
    gather = jax.vmap(lambda t, idx: t[idx])

    def chunk(args):
        qc, iqc, iwc, ci = args
        qpos = ci * DSA_QCHUNK + jnp.arange(DSA_QCHUNK)
        rel = jax.nn.relu(jnp.einsum('bqhd,bsd->bqhs', iqc, ik, preferred_element_type=jnp.float32) * (IDX_DIM ** -0.5))
        score = jnp.einsum('bqhs,bqh->bqs', rel, iwc)
        score = jnp.where(kpos[None, :] <= qpos[:, None], score, -jnp.inf)
        _, sel = lax.top_k(score, n_keep)
        ok = sel <= qpos[None, :, None]
        k_sel = gather(k, sel)
        v_sel = gather(v, sel)
        s = jnp.einsum('bqhd,bqkd->bhqk', qc, k_sel, preferred_element_type=jnp.float32) * scale
        s = jnp.where(ok[:, None], s, MASK_VALUE)
        p = jax.nn.softmax(s, axis=-1)
        return jnp.einsum('bhqk,bqkd->bqhd', p.astype(v.dtype), v_sel)

    def split(t):
        return t.reshape((B, nch, DSA_QCHUNK) + t.shape[2:]).swapaxes(0, 1)

    o = lax.map(chunk, (split(q), split(iq), split(iw), jnp.arange(nch)))
    return o.swapaxes(0, 1).reshape(B, S, H * hd)


def hybrid_token_mixer(h, positions, w_in, mla_q_norm, mla_q_up, mla_kv_norm, mla_kv_up, w_branch, w_out):
    B, S, D = h.shape
    (a_q, a_k, a_v, b_cq, b_ckv, b_kr, c_q, c_k, c_v,
     d_q, d_k, d_v, d_iq, d_ik, d_iw, g) = jnp.split(h @ w_in, IN_SPLITS, axis=-1)
    o_a = dilated_mixture(a_q, a_k, a_v, positions)
    o_b = mla_attention(b_cq, b_ckv, b_kr, positions, mla_q_norm, mla_q_up, mla_kv_norm, mla_kv_up)
    o_c = moba_attention(c_q, c_k, c_v, positions)
    o_d = dsa_attention(d_q, d_k, d_v, d_iq, d_ik, d_iw, positions)
    p_a, p_b, p_c, p_d = jnp.split(w_branch, BRANCH_SPLITS, axis=0)
    gates = jax.nn.sigmoid(g.astype(jnp.float32)).astype(h.dtype).reshape(B, S, N_BRANCH, D)
    y = (gates[:, :, 0] * (o_a @ p_a) + gates[:, :, 1] * (o_b @ p_b)
         + gates[:, :, 2] * (o_c @ p_c) + gates[:, :, 3] * (o_d @ p_d))
    return y @ w_out


def swiglu(h, w_gate, w_up, w_down):
    return (jax.nn.silu(h @ w_gate) * (h @ w_up)) @ w_down


def moe_swiglu(h, w_router, w_gate, w_up, w_down):
    B, S, D = h.shape
    T = B * S
    A = T * MOE_TOP_K
    ht = h.reshape(T, D)
    logits = jnp.matmul(ht, w_router, preferred_element_type=jnp.float32)
    top_logits, top_idx = lax.top_k(logits, MOE_TOP_K)
    top_w = jax.nn.softmax(top_logits, axis=-1)
    e_flat = top_idx.reshape(A)
    tok_flat = jnp.repeat(jnp.arange(T, dtype=jnp.int32), MOE_TOP_K)
    w_flat = top_w.reshape(A)
    order = jnp.argsort(e_flat)
    e_s, tok_s, w_s = e_flat[order], tok_flat[order], w_flat[order]
    counts = jnp.bincount(e_flat, length=N_EXPERTS)
    starts = jnp.cumsum(counts) - counts
    padded = (counts + MOE_ROWS - 1) // MOE_ROWS * MOE_ROWS
    pends = jnp.cumsum(padded)
    pstarts = pends - padded
    dest = pstarts[e_s] + jnp.arange(A) - starts[e_s]
    n_blocks = -(-A // MOE_ROWS) + N_EXPERTS
    P = n_blocks * MOE_ROWS
    x_buf = jnp.zeros((P, D), h.dtype).at[dest].set(ht[tok_s])
    tok_buf = jnp.full((P,), T, jnp.int32).at[dest].set(tok_s)
    w_buf = jnp.zeros((P,), jnp.float32).at[dest].set(w_s)
    block_expert = jnp.minimum(jnp.searchsorted(pends, jnp.arange(n_blocks) * MOE_ROWS, side='right'), N_EXPERTS - 1)

    def expert_block(args):
        xb, e = args
        return swiglu(xb, w_gate[e], w_up[e], w_down[e])

    y_buf = lax.map(expert_block, (x_buf.reshape(n_blocks, MOE_ROWS, D), block_expert)).reshape(P, D)
    out = jnp.zeros((T, D), h.dtype).at[tok_buf].add(y_buf * w_buf[:, None].astype(h.dtype), mode='drop')
    return out.reshape(B, S, D)


def setup_inputs(seed: int = 0) -> dict:
    key = jax.random.key(seed)
    keys = iter(jax.random.split(key, 64))
    D = D_MODEL

    def normal(shape, std):
        return jax.random.normal(next(keys), shape, jnp.float32) * std

    def gain(n):
        return 1.0 + normal((n,), 0.02)

    inputs = {}
    inputs['x'] = normal((BATCH, SEQ, D), 1.0)
    inputs['c'] = normal((BATCH, D), 1.0)
    offsets = jax.random.randint(next(keys), (BATCH, 1), 0, 1024, dtype=jnp.int32)
    inputs['positions'] = offsets + jnp.arange(SEQ, dtype=jnp.int32)[None, :]
    inputs['w_ada'] = normal((D, N_MOD * D), 0.5 * D ** -0.5)
    inputs['b_ada'] = normal((N_MOD * D,), 0.02)
    for layer in range(DEPTH):
        inputs[f'ada_table_{layer}'] = normal((N_MOD, D), 0.1)
        inputs[f'mix_norm_{layer}'] = gain(D)
        inputs[f'w_in_{layer}'] = normal((D, D_IN), D ** -0.5)
        inputs[f'mla_q_norm_{layer}'] = gain(MLA_Q_LORA)
        inputs[f'mla_q_up_{layer}'] = normal((MLA_Q_LORA, MLA_HEADS * (MLA_NOPE + MLA_ROPE)), MLA_Q_LORA ** -0.5)
        inputs[f'mla_kv_norm_{layer}'] = gain(MLA_KV_LORA)
        inputs[f'mla_kv_up_{layer}'] = normal((MLA_KV_LORA, MLA_HEADS * (MLA_NOPE + MLA_V)), MLA_KV_LORA ** -0.5)
        inputs[f'w_branch_{layer}'] = jnp.concatenate([normal((n, D), n ** -0.5) for n in BRANCH_SIZES], axis=0)
        inputs[f'w_out_{layer}'] = normal((D, D), D ** -0.5)
        inputs[f'ffn_norm_{layer}'] = gain(D)
        if layer % 2 == 0:
            inputs[f'ffn_gate_{layer}'] = normal((D, D_FF_DENSE), D ** -0.5)
            inputs[f'ffn_up_{layer}'] = normal((D, D_FF_DENSE), D ** -0.5)
            inputs[f'ffn_down_{layer}'] = normal((D_FF_DENSE, D), D_FF_DENSE ** -0.5)
        else:
            inputs[f'router_{layer}'] = normal((D, N_EXPERTS), D ** -0.5)
            inputs[f'expert_gate_{layer}'] = normal((N_EXPERTS, D, D_FF_EXPERT), D ** -0.5)
            inputs[f'expert_up_{layer}'] = normal((N_EXPERTS, D, D_FF_EXPERT), D ** -0.5)
            inputs[f'expert_down_{layer}'] = normal((N_EXPERTS, D_FF_EXPERT, D), D_FF_EXPERT ** -0.5)
    inputs['final_norm'] = gain(D)
    return inputs


def reference(x, c, positions, w_ada, b_ada,
              ada_table_0, mix_norm_0, w_in_0, mla_q_norm_0, mla_q_up_0, mla_kv_norm_0, mla_kv_up_0,
              w_branch_0, w_out_0, ffn_norm_0, ffn_gate_0, ffn_up_0, ffn_down_0,
              ada_table_1, mix_norm_1, w_in_1, mla_q_norm_1, mla_q_up_1, mla_kv_norm_1, mla_kv_up_1,
              w_branch_1, w_out_1, ffn_norm_1, router_1, expert_gate_1, expert_up_1, expert_down_1,
              final_norm):
    B, S, D = x.shape
    mod_shared = (jax.nn.silu(c) @ w_ada + b_ada).reshape(B, N_MOD, D)
    mixer_params = (
        (ada_table_0, mix_norm_0, w_in_0, mla_q_norm_0, mla_q_up_0, mla_kv_norm_0, mla_kv_up_0, w_branch_0, w_out_0),
        (ada_table_1, mix_norm_1, w_in_1, mla_q_norm_1, mla_q_up_1, mla_kv_norm_1, mla_kv_up_1, w_branch_1, w_out_1),
    )
    ffn_params = (
        (ffn_norm_0, (ffn_gate_0, ffn_up_0, ffn_down_0)),
        (ffn_norm_1, (router_1, expert_gate_1, expert_up_1, expert_down_1)),
    )
    for layer in range(DEPTH):
        ada_table, mix_norm, w_in, qn, qu, kvn, kvu, w_branch, w_out = mixer_params[layer]
        ffn_norm, ffn_w = ffn_params[layer]
        mod = mod_shared + ada_table[None]
        shift_m, scale_m, gate_m, shift_f, scale_f, gate_f = [mod[:, i, None, :] for i in range(N_MOD)]
        h = rms_norm(x, mix_norm) * (1 + scale_m) + shift_m
        x = x + gate_m * hybrid_token_mixer(h, positions, w_in, qn, qu, kvn, kvu, w_branch, w_out)
        h = rms_norm(x, ffn_norm) * (1 + scale_f) + shift_f
        ffn_out = swiglu(h, *ffn_w) if layer % 2 == 0 else moe_swiglu(h, *ffn_w)
        x = x + gate_f * ffn_out
    return rms_norm(x, final_norm)
```

```python
import functools

import numpy as np
import jax
import jax.numpy as jnp
from jax import lax
from jax.experimental import pallas as pl
from jax.experimental.pallas import tpu as pltpu

F32, BF16, I32 = jnp.float32, jnp.bfloat16, jnp.int32

LANES = 128
SUBLANES = 8
VMEM_LIMIT_BYTES = 56 * 1024 * 1024

HEAD_DIM = 128
ROT_DIM = HEAD_DIM // 4
ROPE_THETA = 500000.0
NORM_EPS = 1e-6
MASK_VALUE = -1e30
DIL_GROUPS = ((128, 1), (512, 4), (2048, 16))
DIL_HEADS_PER_GROUP = 4
DIL_HEADS = len(DIL_GROUPS) * DIL_HEADS_PER_GROUP
DIL_WIDTH = DIL_HEADS * HEAD_DIM
DIL_OUT = DIL_HEADS_PER_GROUP * HEAD_DIM
MLA_HEADS = 8
MLA_Q_LORA = 1536
MLA_KV_LORA = 512
MLA_NOPE = 128
MLA_ROPE = 64
MLA_V = 128
MLA_QK_PAD = 256
MOBA_HEADS = 8
MOBA_BLOCK = 256
MOBA_TOPK = 3
DSA_HEADS = 8
DSA_TOPK = 256
IDX_HEADS = 32
IDX_DIM = 64
IDX_ROT = IDX_DIM // 4
N_BRANCH = 4
N_EXPERTS = 8
N_MOD = 6
INT_MIN = -2147483648

DSA_TQ = 128
DSA_TK = 512
MOE_ROWS = 256


def _cparams(n_axes):
    return pltpu.CompilerParams(dimension_semantics=("arbitrary",) * n_axes,
                                vmem_limit_bytes=VMEM_LIMIT_BYTES)


def _tile(n, pref, mult=LANES):
    if n <= pref:
        return n
    t = (pref // mult) * mult
    while t > mult and n % t:
        t -= mult
    assert n % t == 0, (n, pref, mult)
    return t


def _nt_dot(a, b):
    return lax.dot_general(a, b, (((1,), (1,)), ((), ())), preferred_element_type=F32)


def _rope_pattern(group, rot, active):
    half = rot // 2
    freqs = ROPE_THETA ** (-jnp.arange(half, dtype=F32) * (2.0 / rot))
    lane = np.arange(LANES)
    p = lane % group
    on = (lane < active) & (p < rot)
    idx = np.where(on, p % half, 0)
    freq = jnp.where(jnp.asarray(on), freqs[idx], 0.0)
    first = (on & (p < half)).astype(np.float32)
    second = (on & (p >= half)).astype(np.float32)
    pat = jnp.zeros((SUBLANES, LANES), F32)
    return pat.at[0].set(freq).at[1].set(jnp.asarray(first)).at[2].set(jnp.asarray(second))


def _rope_table_kernel(pos_ref, pat_ref, c_ref, sa_ref, sb_ref):
    ang = pos_ref[...] * pat_ref[0:1, :]
    c_ref[...] = jnp.cos(ang)
    sin = jnp.sin(ang)
    sa_ref[...] = -sin * pat_ref[1:2, :]
    sb_ref[...] = sin * pat_ref[2:3, :]


def _rope_tables(pos_b, pat):
    t = pos_b.shape[0]
    ts = _tile(t, 1024)
    spec = pl.BlockSpec((ts, LANES), lambda i: (i, 0))
    sds = jax.ShapeDtypeStruct((t, LANES), F32)
    return pl.pallas_call(
        _rope_table_kernel, grid=(t // ts,),
        in_specs=[spec, pl.BlockSpec((SUBLANES, LANES), lambda i: (0, 0))],
        out_specs=[spec, spec, spec], out_shape=[sds, sds, sds],
        compiler_params=_cparams(1))(pos_b, pat)


def _apply_rope(x, c, sa, sb, half):
    return x * c + pltpu.roll(x, LANES - half, 1) * sa + pltpu.roll(x, half, 1) * sb


def _ada_kernel(c_ref, w_ref, b_ref, t0_ref, t1_ref, o0_ref, o1_ref):
    c = c_ref[...]
    a = (c * jax.nn.sigmoid(c)).astype(BF16)
    acc = jnp.dot(a, w_ref[...].astype(BF16), preferred_element_type=F32) + b_ref[...]
    o0_ref[...] = acc + t0_ref[...]
    o1_ref[...] = acc + t1_ref[...]


def _ada_mod(c, w_ada, b_ada, table0, table1):
    b, d = c.shape
    n = w_ada.shape[1]
    bp = -(-b // SUBLANES) * SUBLANES
    cp = jnp.zeros((bp, d), F32).at[:b].set(c)
    tn = _tile(n, 512)
    row = pl.BlockSpec((1, tn), lambda j: (0, j))
    out = pl.BlockSpec((bp, tn), lambda j: (0, j))
    sds = jax.ShapeDtypeStruct((bp, n), F32)
    m0, m1 = pl.pallas_call(
        _ada_kernel, grid=(n // tn,),
        in_specs=[pl.BlockSpec((bp, d), lambda j: (0, 0)), pl.BlockSpec((d, tn), lambda j: (0, j)), row, row, row],
        out_specs=[out, out], out_shape=[sds, sds], compiler_params=_cparams(1),
    )(cp, w_ada, b_ada.reshape(1, n), table0.reshape(1, n), table1.reshape(1, n))

    def pack(m):
        m = m[:b].reshape(b, N_MOD, d)
        return jnp.concatenate([m, jnp.zeros((b, SUBLANES - N_MOD, d), F32)], axis=1)
    return pack(m0), pack(m1)


def _norm_kernel(*refs, res_row, shift_row, scale_row, with_logits, h_dtype):
    it = iter(refs)
    x_ref = next(it)
    y_ref, rmod_ref = (next(it), next(it)) if res_row is not None else (None, None)
    mod_ref, gain_ref = next(it), next(it)
    wr_ref = next(it) if with_logits else None
    xo_ref = next(it) if res_row is not None else None
    h_ref = next(it)
    lg_ref = next(it) if with_logits else None

    x = x_ref[...]
    if res_row is not None:
        x = x + rmod_ref[res_row:res_row + 1, :] * y_ref[...].astype(F32)
        xo_ref[...] = x
    y = x * lax.rsqrt(jnp.mean(x * x, axis=-1, keepdims=True) + NORM_EPS) * gain_ref[...]
    h = y * (1.0 + mod_ref[scale_row:scale_row + 1, :]) + mod_ref[shift_row:shift_row + 1, :]
    h_ref[...] = h.astype(h_dtype)
    if with_logits:
        lg_ref[...] = jnp.dot(h.astype(BF16), wr_ref[...], preferred_element_type=F32)


def _norm_mod(x, mod, gain, seq, *, shift_row, scale_row, y=None, res_mod=None, res_row=None, w_router=None,
              h_dtype=BF16):
    t, d = x.shape
    ts = _tile(seq, 256, SUBLANES)
    per_b = seq // ts
    blk = pl.BlockSpec((ts, d), lambda i: (i, 0))
    mod_spec = pl.BlockSpec((None, SUBLANES, d), lambda i: (i // per_b, 0, 0))
    in_specs, args = [blk], [x]
    if res_row is not None:
        in_specs += [blk, mod_spec]
        args += [y, res_mod]
    in_specs += [mod_spec, pl.BlockSpec((1, d), lambda i: (0, 0))]
    args += [mod, gain.reshape(1, d)]
    if w_router is not None:
        in_specs.append(pl.BlockSpec((d, LANES), lambda i: (0, 0)))
        args.append(w_router)
    out_specs, out_shape = [], []
    if res_row is not None:
        out_specs.append(blk)
        out_shape.append(jax.ShapeDtypeStruct((t, d), F32))
    out_specs.append(blk)
    out_shape.append(jax.ShapeDtypeStruct((t, d), h_dtype))
    if w_router is not None:
        out_specs.append(pl.BlockSpec((ts, LANES), lambda i: (i, 0)))
        out_shape.append(jax.ShapeDtypeStruct((t, LANES), F32))
    kern = functools.partial(_norm_kernel, res_row=res_row, shift_row=shift_row, scale_row=scale_row,
                             with_logits=w_router is not None, h_dtype=h_dtype)
    return pl.pallas_call(kern, grid=(t // ts,), in_specs=in_specs, out_specs=out_specs, out_shape=out_shape,
                          compiler_params=_cparams(1))(*args)


def _mm_kernel(*refs, epi, half, res_row):
    a_ref, w_ref = refs[0], refs[1]
    o_ref = refs[-1]
    acc = jnp.dot(a_ref[...], w_ref[...], preferred_element_type=F32)
    if epi == "plain":
        o_ref[...] = acc.astype(o_ref.dtype)
    elif epi == "sigmoid":
        o_ref[...] = jax.nn.sigmoid(acc).astype(o_ref.dtype)
    elif epi == "rope":
        c, sa, sb = refs[2][...], refs[3][...], refs[4][...]
        for ch in range(acc.shape[1] // LANES):
            sl = slice(ch * LANES, (ch + 1) * LANES)
            o_ref[:, sl] = _apply_rope(acc[:, sl], c, sa, sb, half).astype(o_ref.dtype)
    elif epi == "resid":
        xres_ref, mod_ref = refs[2], refs[3]
        o_ref[...] = xres_ref[...] + mod_ref[res_row:res_row + 1, :] * acc
    else:
        raise ValueError(epi)


def _matmul(a, w, *, out_dtype, tn_pref, epi="plain", rope=None, half=0, xres=None, mod=None, res_row=None, seq=None):
    m, k = a.shape
    n = w.shape[1]
    tm = _tile(seq if seq is not None else m, 1024, SUBLANES)
    tn = _tile(n, tn_pref)
    in_specs = [pl.BlockSpec((tm, k), lambda i, j: (i, 0)), pl.BlockSpec((k, tn), lambda i, j: (0, j))]
    args = [a, w]
    if epi == "rope":
        in_specs += [pl.BlockSpec((tm, LANES), lambda i, j: (i, 0))] * 3
        args += list(rope)
    if epi == "resid":
        per_b = seq // tm
        in_specs += [pl.BlockSpec((tm, tn), lambda i, j: (i, j)),
                     pl.BlockSpec((None, SUBLANES, tn), lambda i, j: (i // per_b, 0, j))]
        args += [xres, mod]
    kern = functools.partial(_mm_kernel, epi=epi, half=half, res_row=res_row)
    return pl.pallas_call(
        kern, grid=(m // tm, n // tn), in_specs=in_specs,
        out_specs=pl.BlockSpec((tm, tn), lambda i, j: (i, j)),
        out_shape=jax.ShapeDtypeStruct((m, n), out_dtype), compiler_params=_cparams(2))(*args)


def _dil_kernel(q_ref, kp_ref, ko_ref, vp_ref, vo_ref, o_ref, lse_ref, *, scale):
    i = pl.program_id(3)
    q = q_ref[...]
    w = q.shape[0]
    row = lax.broadcasted_iota(I32, (w, w), 0)
    col = lax.broadcasted_iota(I32, (w, w), 1)
    s_o = jnp.where(col <= row, _nt_dot(q, ko_ref[...]) * scale, MASK_VALUE)
    s_p = jnp.where(col >= row, _nt_dot(q, kp_ref[...]) * scale, MASK_VALUE)
    s_p = jnp.where(i > 0, s_p, MASK_VALUE)
    m = jnp.maximum(jnp.max(s_o, axis=1, keepdims=True), jnp.max(s_p, axis=1, keepdims=True))
    p_o = jnp.exp(s_o - m)
    p_p = jnp.exp(s_p - m)
    l = jnp.sum(p_o, axis=1, keepdims=True) + jnp.sum(p_p, axis=1, keepdims=True)
    o = (jnp.dot(p_o.astype(BF16), vo_ref[...], preferred_element_type=F32)
         + jnp.dot(p_p.astype(BF16), vp_ref[...], preferred_element_type=F32))
    o_ref[...] = o / l
    lse_ref[...] = jnp.broadcast_to(m + jnp.log(l), o.shape)


def _dilated_group(rot, plain, bsz, seq, g, q_col, k_col, v_col):
    window, dil = DIL_GROUPS[g]
    wsub = window // dil
    n = seq // dil
    assert wsub == HEAD_DIM and n % wsub == 0
    nb = n // wsub
    rc, pc = rot.shape[1] // LANES, plain.shape[1] // LANES
    rot_v = rot.reshape(bsz, n, dil * rot.shape[1])
    plain_v = plain.reshape(bsz, n, dil * plain.shape[1])
    h0 = g * DIL_HEADS_PER_GROUP
    blk = (None, wsub, LANES)

    def own(cols, col0):
        return pl.BlockSpec(blk, lambda b, r, h, i: (b, i, r * cols + col0 + h0 + h))

    def prev(cols, col0):
        return pl.BlockSpec(blk, lambda b, r, h, i: (b, jnp.maximum(i - 1, 0), r * cols + col0 + h0 + h))

    out_spec = pl.BlockSpec(blk, lambda b, r, h, i: (b, i, r * DIL_HEADS_PER_GROUP + h))
    sds = jax.ShapeDtypeStruct((bsz, n, dil * DIL_OUT), F32)
    o, lse = pl.pallas_call(
        functools.partial(_dil_kernel, scale=HEAD_DIM ** -0.5),
        grid=(bsz, dil, DIL_HEADS_PER_GROUP, nb),
        in_specs=[own(rc, q_col), prev(rc, k_col), own(rc, k_col), prev(pc, v_col), own(pc, v_col)],
        out_specs=[out_spec, out_spec], out_shape=[sds, sds], compiler_params=_cparams(4),
    )(rot_v, rot_v, rot_v, plain_v, plain_v)
    return o.reshape(bsz * seq, DIL_OUT), lse.reshape(bsz * seq, DIL_OUT)


def _dil_combine_kernel(o0, o1, o2, l0, l1, l2, out_ref):
    a, b, c = l0[...], l1[...], l2[...]
    m = jnp.maximum(jnp.maximum(a, b), c)
    ea, eb, ec = jnp.exp(a - m), jnp.exp(b - m), jnp.exp(c - m)
    out_ref[...] = ((ea * o0[...] + eb * o1[...] + ec * o2[...]) / (ea + eb + ec)).astype(out_ref.dtype)


def _dilated_mixture(rot, plain, bsz, seq, q_col, k_col, v_col):
    outs, lses = zip(*[_dilated_group(rot, plain, bsz, seq, g, q_col, k_col, v_col) for g in range(len(DIL_GROUPS))])
    t = bsz * seq
    ts = _tile(t, 1024, SUBLANES)
    spec = pl.BlockSpec((ts, DIL_OUT), lambda i: (i, 0))
    return pl.pallas_call(
        _dil_combine_kernel, grid=(t // ts,), in_specs=[spec] * 6, out_specs=spec,
        out_shape=jax.ShapeDtypeStruct((t, DIL_OUT), BF16), compiler_params=_cparams(1))(*outs, *lses)


def _rms(x, gain):
    return x * lax.rsqrt(jnp.mean(x * x, axis=-1, keepdims=True) + NORM_EPS) * gain


def _mla_q_kernel(cq_ref, gain_ref, w_ref, c_ref, sa_ref, sb_ref, o_ref):
    cqn = _rms(cq_ref[...].astype(F32), gain_ref[...]).astype(BF16)
    acc = jnp.dot(cqn, w_ref[...], preferred_element_type=F32)
    c, sa, sb = c_ref[...], sa_ref[...], sb_ref[...]
    for ch in range(acc.shape[1] // LANES):
        sl = slice(ch * LANES, (ch + 1) * LANES)
        x = acc[:, sl]
        if ch % 2 == 1:
            x = _apply_rope(x, c, sa, sb, MLA_ROPE // 2)
        o_ref[:, sl] = x.astype(o_ref.dtype)


def _mla_kv_kernel(ckv_ref, gain_ref, w_ref, kr_ref, c_ref, sa_ref, sb_ref, k_ref, v_ref):
    ckvn = _rms(ckv_ref[...].astype(F32), gain_ref[...]).astype(BF16)
    acc = jnp.dot(ckvn, w_ref[...], preferred_element_type=F32)
    kpe = _apply_rope(kr_ref[...].astype(F32), c_ref[...], sa_ref[...], sb_ref[...], MLA_ROPE // 2).astype(k_ref.dtype)
    hw = MLA_HEADS * MLA_NOPE
    for h in range(MLA_HEADS):
        k_ref[:, h * MLA_QK_PAD:h * MLA_QK_PAD + MLA_NOPE] = acc[:, h * MLA_NOPE:(h + 1) * MLA_NOPE].astype(k_ref.dtype)
        k_ref[:, h * MLA_QK_PAD + MLA_NOPE:(h + 1) * MLA_QK_PAD] = kpe
    v_ref[...] = acc[:, hw:].astype(v_ref.dtype)


def _mla_project(plain, cq_col, ckv_col, kr_col, q_gain, wq, kv_gain, wkv, rope):
    t = plain.shape[0]
    ts = _tile(t, 512, SUBLANES)
    tab = pl.BlockSpec((ts, LANES), lambda i: (i, 0))
    nq = MLA_HEADS * MLA_QK_PAD
    q_cat = pl.pallas_call(
        _mla_q_kernel, grid=(t // ts,),
        in_specs=[pl.BlockSpec((ts, MLA_Q_LORA), lambda i: (i, cq_col * LANES // MLA_Q_LORA)),
                  pl.BlockSpec((1, MLA_Q_LORA), lambda i: (0, 0)),
                  pl.BlockSpec((MLA_Q_LORA, nq), lambda i: (0, 0)), tab, tab, tab],
        out_specs=pl.BlockSpec((ts, nq), lambda i: (i, 0)),
        out_shape=jax.ShapeDtypeStruct((t, nq), BF16), compiler_params=_cparams(1),
    )(plain, q_gain.reshape(1, -1), wq, *rope)
    nv = MLA_HEADS * MLA_V
    k_cat, v = pl.pallas_call(
        _mla_kv_kernel, grid=(t // ts,),
        in_specs=[pl.BlockSpec((ts, MLA_KV_LORA), lambda i: (i, ckv_col * LANES // MLA_KV_LORA)),
                  pl.BlockSpec((1, MLA_KV_LORA), lambda i: (0, 0)),
                  pl.BlockSpec((MLA_KV_LORA, MLA_HEADS * MLA_NOPE + nv), lambda i: (0, 0)),
                  pl.BlockSpec((ts, LANES), lambda i: (i, kr_col)), tab, tab, tab],
        out_specs=[pl.BlockSpec((ts, nq), lambda i: (i, 0)), pl.BlockSpec((ts, nv), lambda i: (i, 0))],
        out_shape=[jax.ShapeDtypeStruct((t, nq), BF16), jax.ShapeDtypeStruct((t, nv), BF16)],
        compiler_params=_cparams(1),
    )(plain, kv_gain.reshape(1, -1), wkv, plain, *rope)
    return q_cat, k_cat, v


def _flash_kernel(*refs, scale, moba):
    if moba:
        q_ref, k_ref, v_ref, km_ref, o_ref, m_sc, l_sc, acc_sc, sel_sc = refs
    else:
        q_ref, k_ref, v_ref, o_ref, m_sc, l_sc, acc_sc = refs
    qi, step = pl.program_id(2), pl.program_id(3)
    kv = qi - step
    tq, tk = q_ref.shape[0], k_ref.shape[0]

    @pl.when(step == 0)
    def _init():
        m_sc[...] = jnp.full(m_sc.shape, MASK_VALUE, F32)
        l_sc[...] = jnp.zeros(l_sc.shape, F32)
        acc_sc[...] = jnp.zeros(acc_sc.shape, F32)
        if moba:
            lane = lax.broadcasted_iota(I32, (tq, LANES), 1)
            own = (qi * tq + lax.broadcasted_iota(I32, (tq, LANES), 0)) // MOBA_BLOCK
            valid = lane < own
            g = jnp.where(valid, _nt_dot(q_ref[...], km_ref[...]), MASK_VALUE)
            sel = jnp.zeros((tq, LANES), F32)
            for _ in range(MOBA_TOPK):
                mx = jnp.max(g, axis=1, keepdims=True)
                first = jnp.min(jnp.where(g == mx, lane, LANES), axis=1, keepdims=True)
                pick = lane == first
                sel = jnp.where(pick, 1.0, sel)
                g = jnp.where(pick, -jnp.inf, g)
            sel_sc[...] = jnp.where(valid, sel, 0.0)

    @pl.when(step <= qi)
    def _step():
        s = _nt_dot(q_ref[...], k_ref[...]) * scale
        rowg = qi * tq + lax.broadcasted_iota(I32, (tq, tk), 0)
        colg = kv * tk + lax.broadcasted_iota(I32, (tq, tk), 1)
        ok = colg <= rowg
        if moba:
            lane = lax.broadcasted_iota(I32, (tq, LANES), 1)
            sel = sel_sc[...]
            nsub = tk // MOBA_BLOCK
            flags = []
            for c in range(nsub):
                f = jnp.sum(jnp.where(lane == kv * nsub + c, sel, 0.0), axis=1, keepdims=True)
                flags.append(jnp.broadcast_to(f, (tq, MOBA_BLOCK)))
            flag = flags[0] if nsub == 1 else jnp.concatenate(flags, axis=1)
            same = (colg // MOBA_BLOCK) == (rowg // MOBA_BLOCK)
            ok = jnp.logical_and(ok, jnp.logical_or(same, flag > 0.0))
        s = jnp.where(ok, s, MASK_VALUE)
        m_prev = m_sc[:, 0:1]
        m_new = jnp.maximum(m_prev, jnp.max(s, axis=1, keepdims=True))
        alpha = jnp.exp(m_prev - m_new)
        p = jnp.exp(s - m_new)
        l_sc[...] = jnp.broadcast_to(alpha * l_sc[:, 0:1] + jnp.sum(p, axis=1, keepdims=True), l_sc.shape)
        acc_sc[...] = alpha * acc_sc[...] + jnp.dot(p.astype(BF16), v_ref[...], preferred_element_type=F32)
        m_sc[...] = jnp.broadcast_to(m_new, m_sc.shape)

    @pl.when(step == pl.num_programs(3) - 1)
    def _fin():
        o_ref[...] = (acc_sc[...] / l_sc[:, 0:1]).astype(o_ref.dtype)


def _flash(q, k, v, bsz, seq, heads, dq, dv, q_col, k_col, v_col, scale, kmean=None):
    t = bsz * seq
    tq = _tile(seq, 1024, MOBA_BLOCK)
    nq = seq // tq
    in_specs = [pl.BlockSpec((tq, dq), lambda b, h, i, s: (b * nq + i, q_col + h)),
                pl.BlockSpec((tq, dq), lambda b, h, i, s: (b * nq + jnp.maximum(i - s, 0), k_col + h)),
                pl.BlockSpec((tq, dv), lambda b, h, i, s: (b * nq + jnp.maximum(i - s, 0), v_col + h))]
    args = [q, k, v]
    scratch = [pltpu.VMEM((tq, LANES), F32), pltpu.VMEM((tq, LANES), F32), pltpu.VMEM((tq, dv), F32)]
    if kmean is not None:
        in_specs.append(pl.BlockSpec((None, None, LANES, HEAD_DIM), lambda b, h, i, s: (b, h, 0, 0)))
        args.append(kmean)
        scratch.append(pltpu.VMEM((tq, LANES), F32))
    return pl.pallas_call(
        functools.partial(_flash_kernel, scale=scale, moba=kmean is not None),
        grid=(bsz, heads, nq, nq), in_specs=in_specs,
        out_specs=pl.BlockSpec((tq, dv), lambda b, h, i, s: (b * nq + i, h)),
        out_shape=jax.ShapeDtypeStruct((t, heads * dv), BF16),
        scratch_shapes=scratch, compiler_params=_cparams(4))(*args)


def _kmean_kernel(k_ref, o_ref):
    s = k_ref.shape[0]
    nb = s // MOBA_BLOCK
    k = k_ref[...].astype(F32).reshape(nb, MOBA_BLOCK, HEAD_DIM)
    mean = jnp.sum(k, axis=1) * (1.0 / MOBA_BLOCK)
    full = jnp.concatenate([mean, jnp.zeros((LANES - nb, HEAD_DIM), F32)], axis=0)
    o_ref[...] = full.astype(o_ref.dtype)


def _moba_kmean(rot, bsz, seq, k_col):
    assert seq % MOBA_BLOCK == 0 and seq // MOBA_BLOCK <= LANES
    return pl.pallas_call(
        _kmean_kernel, grid=(bsz, MOBA_HEADS),
        in_specs=[pl.BlockSpec((seq, HEAD_DIM), lambda b, h: (b, k_col + h))],
        out_specs=pl.BlockSpec((None, None, LANES, HEAD_DIM), lambda b, h: (b, h, 0, 0)),
        out_shape=jax.ShapeDtypeStruct((bsz, MOBA_HEADS, LANES, HEAD_DIM), BF16),
        compiler_params=_cparams(2))(rot)


def _dsa_kernel(q_ref, iq_ref, iw_ref, ikbd_ref, k_ref, v_ref, o_ref, key_sc, m_sc, l_sc, acc_sc, *, n_keep, scale, wscale):
    c = pl.program_id(1)
    tq, tk = DSA_TQ, DSA_TK
    nkt = (c * tq) // tk + 1
    rowg = c * tq + lax.broadcasted_iota(I32, (tq, tk), 0)
    col = lax.broadcasted_iota(I32, (tq, tk), 1)
    w = iw_ref[...] * wscale

    def score_tile(kt, carry):
        ikb = ikbd_ref[kt]
        sc = jnp.zeros((tq, tk), F32)
        for j in range(IDX_HEADS // 2):
            rel = jnp.maximum(_nt_dot(iq_ref[:, j * LANES:(j + 1) * LANES], ikb), 0.0)
            sc = sc + rel[:, :tk] * w[:, 2 * j:2 * j + 1] + rel[:, tk:] * w[:, 2 * j + 1:2 * j + 2]
        bits = lax.bitcast_convert_type(sc, I32)
        key = bits ^ ((bits >> 31) & 0x7FFFFFFF)
        key_sc[kt] = jnp.where(kt * tk + col <= rowg, key, INT_MIN)
        return carry
    lax.fori_loop(0, nkt, score_tile, 0)

    def count_ge(cand):
        def body(kt, acc):
            ge = jnp.where(key_sc[kt] >= cand, 1.0, 0.0)
            for j in range(tk // LANES):
                acc = acc + ge[:, j * LANES:(j + 1) * LANES]
            return acc
        acc = lax.fori_loop(0, nkt, body, jnp.zeros((tq, LANES), F32))
        return jnp.sum(acc, axis=1, keepdims=True)

    keep = jnp.float32(n_keep)
    thr = jnp.where(count_ge(jnp.zeros((tq, 1), I32)) >= keep, 0, INT_MIN).astype(I32)

    def search(it, thr):
        cand = thr | (jnp.int32(1) << (30 - it))
        return jnp.where(count_ge(cand) >= keep, cand, thr)
    thr = lax.fori_loop(0, 31, search, thr)
    thr = jnp.maximum(thr, INT_MIN + 1)

    m_sc[...] = jnp.full(m_sc.shape, MASK_VALUE, F32)
    l_sc[...] = jnp.zeros(l_sc.shape, F32)
    acc_sc[...] = jnp.zeros(acc_sc.shape, F32)

    def attend(kt, carry):
        sel = key_sc[kt] >= thr
        ks = k_ref[pl.ds(pl.multiple_of(kt * tk, tk), tk), :]
        vs = v_ref[pl.ds(pl.multiple_of(kt * tk, tk), tk), :]
        for h in range(DSA_HEADS):
            s = jnp.where(sel, _nt_dot(q_ref[:, h * HEAD_DIM:(h + 1) * HEAD_DIM], ks) * scale, MASK_VALUE)
            m_prev = m_sc[h][:, 0:1]
            m_new = jnp.maximum(m_prev, jnp.max(s, axis=1, keepdims=True))
            alpha = jnp.exp(m_prev - m_new)
            p = jnp.exp(s - m_new)
            l_sc[h] = jnp.broadcast_to(alpha * l_sc[h][:, 0:1] + jnp.sum(p, axis=1, keepdims=True), (tq, LANES))
            acc_sc[h] = alpha * acc_sc[h] + jnp.dot(p.astype(BF16), vs, preferred_element_type=F32)
            m_sc[h] = jnp.broadcast_to(m_new, (tq, LANES))
        return carry
    lax.fori_loop(0, nkt, attend, 0)

    for h in range(DSA_HEADS):
        o_ref[:, h * HEAD_DIM:(h + 1) * HEAD_DIM] = (acc_sc[h] / l_sc[h][:, 0:1]).astype(o_ref.dtype)


def _dsa_attention(rot, plain, iq, ik, iw, bsz, seq, q_col, k_col, v_col):
    t = bsz * seq
    assert seq % DSA_TK == 0
    nkt = seq // DSA_TK
    nch = seq // DSA_TQ
    ika = ik.astype(BF16).reshape(bsz, nkt, DSA_TK, LANES)
    ikbd = jnp.concatenate([ika, jnp.roll(ika, IDX_DIM, axis=-1)], axis=2)
    n_keep = min(DSA_TOPK, seq // 4)
    qw = DSA_HEADS * HEAD_DIM
    iqw = IDX_HEADS * IDX_DIM
    kern = functools.partial(_dsa_kernel, n_keep=n_keep, scale=HEAD_DIM ** -0.5,
                             wscale=(IDX_HEADS ** -0.5) * (IDX_DIM ** -0.5))
    return pl.pallas_call(
        kern, grid=(bsz, nch),
        in_specs=[pl.BlockSpec((DSA_TQ, qw), lambda b, c: (b * nch + c, q_col * HEAD_DIM // qw)),
                  pl.BlockSpec((DSA_TQ, iqw), lambda b, c: (b * nch + c, 0)),
                  pl.BlockSpec((DSA_TQ, LANES), lambda b, c: (b * nch + c, 0)),
                  pl.BlockSpec((None, nkt, 2 * DSA_TK, LANES), lambda b, c: (b, 0, 0, 0)),
                  pl.BlockSpec((seq, HEAD_DIM), lambda b, c: (b, k_col)),
                  pl.BlockSpec((seq, HEAD_DIM), lambda b, c: (b, v_col))],
        out_specs=pl.BlockSpec((DSA_TQ, qw), lambda b, c: (b * nch + c, 0)),
        out_shape=jax.ShapeDtypeStruct((t, qw), BF16),
        scratch_shapes=[pltpu.VMEM((nkt, DSA_TQ, DSA_TK), I32),
                        pltpu.VMEM((DSA_HEADS, DSA_TQ, LANES), F32), pltpu.VMEM((DSA_HEADS, DSA_TQ, LANES), F32),
                        pltpu.VMEM((DSA_HEADS, DSA_TQ, HEAD_DIM), F32)],
        compiler_params=_cparams(2))(rot, iq, iw, ikbd, rot, plain)


def _branch_kernel(oa, ob, oc, od, pa, pb, pc, pd, ga, gb, gc, gd, y_ref):
    y = None
    for o, p, g in ((oa, pa, ga), (ob, pb, gb), (oc, pc, gc), (od, pd, gd)):
        term = g[...].astype(F32) * jnp.dot(o[...], p[...], preferred_element_type=F32)
        y = term if y is None else y + term
    y_ref[...] = y.astype(y_ref.dtype)


def _branch_merge(outs, projs, gates, d):
    t = outs[0].shape[0]
    tm = _tile(t, 1024, SUBLANES)
    tn = _tile(d, 512)
    nj = d // tn
    in_specs = [pl.BlockSpec((tm, o.shape[1]), lambda i, j: (i, 0)) for o in outs]
    in_specs += [pl.BlockSpec((p.shape[0], tn), lambda i, j: (0, j)) for p in projs]
    in_specs += [pl.BlockSpec((tm, tn), functools.partial(lambda i, j, k: (i, k * nj + j), k=k)) for k in range(N_BRANCH)]
    return pl.pallas_call(
        _branch_kernel, grid=(t // tm, nj), in_specs=in_specs,
        out_specs=pl.BlockSpec((tm, tn), lambda i, j: (i, j)),
        out_shape=jax.ShapeDtypeStruct((t, d), BF16), compiler_params=_cparams(2),
    )(*outs, *projs, gates, gates, gates, gates)


def _ffn_kernel(*refs, n_prefetch):
    pends_ref = refs[0] if n_prefetch else None
    x_ref, wg_ref, wu_ref, wd_ref, o_ref, acc_sc = refs[n_prefetch:n_prefetch + 6]
    f = pl.program_id(1)

    def compute():
        if x_ref.dtype == BF16:
            x = x_ref[...]
        else:
            xb_sc = refs[n_prefetch + 6]

            @pl.when(f == 0)
            def _():
                xb_sc[...] = x_ref[...].astype(BF16)
            x = xb_sc[...]
        g = jnp.dot(x, wg_ref[...], preferred_element_type=F32)
        u = jnp.dot(x, wu_ref[...], preferred_element_type=F32)
        a = (g * jax.nn.sigmoid(g) * u).astype(BF16)
        part = jnp.dot(a, wd_ref[...], preferred_element_type=F32)

        @pl.when(f == 0)
        def _():
            acc_sc[...] = part

        @pl.when(f > 0)
        def _():
            acc_sc[...] += part

        @pl.when(f == pl.num_programs(1) - 1)
        def _():
            o_ref[...] = acc_sc[...].astype(o_ref.dtype)

    if n_prefetch:
        used = pl.program_id(0) * x_ref.shape[0] < pends_ref[N_EXPERTS - 1]
        pl.when(used)(compute)

        @pl.when(jnp.logical_and(jnp.logical_not(used), f == pl.num_programs(1) - 1))
        def _():
            o_ref[...] = jnp.zeros(o_ref.shape, o_ref.dtype)
    else:
        compute()


def _ffn_dense(h, wg, wu, wd):
    t, d = h.shape
    ff = wg.shape[1]
    tm = _tile(t, 512, SUBLANES)
    tf = _tile(ff, 256)
    return pl.pallas_call(
        functools.partial(_ffn_kernel, n_prefetch=0), grid=(t // tm, ff // tf),
        in_specs=[pl.BlockSpec((tm, d), lambda i, f: (i, 0)), pl.BlockSpec((d, tf), lambda i, f: (0, f)),
                  pl.BlockSpec((d, tf), lambda i, f: (0, f)), pl.BlockSpec((tf, d), lambda i, f: (f, 0))],
        out_specs=pl.BlockSpec((tm, d), lambda i, f: (i, 0)),
        out_shape=jax.ShapeDtypeStruct((t, d), BF16),
        scratch_shapes=[pltpu.VMEM((tm, d), F32)],
        compiler_params=_cparams(2))(h, wg, wu, wd)


def _ffn_experts(xbuf, pends, wg, wu, wd, rows):
    p, d = xbuf.shape
    ff = wg.shape[2]
    tf = _tile(ff, 512)

    def expert(i, pends_ref):
        e = jnp.int32(0)
        for j in range(N_EXPERTS - 1):
            e = e + (i * rows >= pends_ref[j]).astype(I32)
        return e

    grid_spec = pltpu.PrefetchScalarGridSpec(
        num_scalar_prefetch=1, grid=(p // rows, ff // tf),
        in_specs=[pl.BlockSpec((rows, d), lambda i, f, pe: (i, 0)),
                  pl.BlockSpec((None, d, tf), lambda i, f, pe: (expert(i, pe), 0, f)),
                  pl.BlockSpec((None, d, tf), lambda i, f, pe: (expert(i, pe), 0, f)),
                  pl.BlockSpec((None, tf, d), lambda i, f, pe: (expert(i, pe), f, 0))],
        out_specs=pl.BlockSpec((rows, d), lambda i, f, pe: (i, 0)),
        scratch_shapes=[pltpu.VMEM((rows, d), F32), pltpu.VMEM((rows, d), BF16)])
    return pl.pallas_call(
        functools.partial(_ffn_kernel, n_prefetch=1), grid_spec=grid_spec,
        out_shape=jax.ShapeDtypeStruct((p, d), F32), compiler_params=_cparams(2))(pends, xbuf, wg, wu, wd)


def _route_kernel(lg_ref, info_ref, w_ref, cnt_ref, carry_sc):
    i = pl.program_id(0)

    @pl.when(i == 0)
    def _():
        carry_sc[...] = jnp.zeros(carry_sc.shape, F32)

    tr = lg_ref.shape[0]
    lane = lax.broadcasted_iota(I32, (tr, LANES), 1)
    lg = jnp.where(lane < N_EXPERTS, lg_ref[...], -jnp.inf)
    m1 = jnp.max(lg, axis=1, keepdims=True)
    i1 = jnp.min(jnp.where(lg == m1, lane, LANES), axis=1, keepdims=True)
    lg2 = jnp.where(lane == i1, -jnp.inf, lg)
    m2 = jnp.max(lg2, axis=1, keepdims=True)
    i2 = jnp.min(jnp.where(lg2 == m2, lane, LANES), axis=1, keepdims=True)
    e = jnp.exp(m2 - m1)
    w1 = 1.0 / (1.0 + e)
    w2 = e * w1
    oh = jnp.where(jnp.logical_or(lane == i1, lane == i2), 1.0, 0.0)
    tri = jnp.where(lax.broadcasted_iota(I32, (tr, tr), 1) < lax.broadcasted_iota(I32, (tr, tr), 0), 1.0, 0.0)
    before = jnp.dot(tri.astype(BF16), oh.astype(BF16), preferred_element_type=F32) + carry_sc[0:1, :]
    r1 = jnp.sum(jnp.where(lane == i1, before, 0.0), axis=1, keepdims=True).astype(I32)
    r2 = jnp.sum(jnp.where(lane == i2, before, 0.0), axis=1, keepdims=True).astype(I32)
    carry_sc[0:1, :] = carry_sc[0:1, :] + jnp.sum(oh, axis=0, keepdims=True)
    info_ref[...] = jnp.where(lane == 0, i1, jnp.where(lane == 1, i2, jnp.where(lane == 2, r1, jnp.where(lane == 3, r2, 0))))
    w_ref[...] = jnp.where(lane == 0, w1, jnp.where(lane == 1, w2, 0.0))
    cnt_ref[...] = carry_sc[...]


def _route(logits):
    t = logits.shape[0]
    tr = _tile(t, 256, SUBLANES)
    blk = pl.BlockSpec((tr, LANES), lambda i: (i, 0))
    return pl.pallas_call(
        _route_kernel, grid=(t // tr,), in_specs=[blk],
        out_specs=[blk, blk, pl.BlockSpec((SUBLANES, LANES), lambda i: (0, 0))],
        out_shape=[jax.ShapeDtypeStruct((t, LANES), I32), jax.ShapeDtypeStruct((t, LANES), F32),
                   jax.ShapeDtypeStruct((SUBLANES, LANES), F32)],
        scratch_shapes=[pltpu.VMEM((SUBLANES, LANES), F32)], compiler_params=_cparams(1))(logits)


def _dispatch_kernel(e_ref, r_ref, ps_ref, h_hbm, xz_hbm, xbuf_hbm, sem, *, per_step):
    del xz_hbm
    base = pl.program_id(0) * per_step

    def copy(a):
        dst = ps_ref[e_ref[a]] + r_ref[a]
        return pltpu.make_async_copy(h_hbm.at[pl.ds(lax.shift_right_logical(a, 1), 1)], xbuf_hbm.at[pl.ds(dst, 1)], sem)

    def start(r, carry):
        copy(base + r).start()
        return carry
    lax.fori_loop(0, per_step, start, 0)

    def wait(r, carry):
        copy(base + r).wait()
        return carry
    lax.fori_loop(0, per_step, wait, 0)


def _dispatch(h, e_flat, r_flat, pstarts, p_rows):
    t, d = h.shape
    per_step = _tile(2 * t, 512, 2)
    grid_spec = pltpu.PrefetchScalarGridSpec(
        num_scalar_prefetch=3, grid=(2 * t // per_step,),
        in_specs=[pl.BlockSpec(memory_space=pl.ANY), pl.BlockSpec(memory_space=pl.ANY)],
        out_specs=pl.BlockSpec(memory_space=pl.ANY),
        scratch_shapes=[pltpu.SemaphoreType.DMA(())])
    return pl.pallas_call(
        functools.partial(_dispatch_kernel, per_step=per_step), grid_spec=grid_spec,
        out_shape=jax.ShapeDtypeStruct((p_rows, d), h.dtype),
        input_output_aliases={4: 0}, compiler_params=_cparams(1),
    )(e_flat, r_flat, pstarts, h, jnp.zeros((p_rows, d), h.dtype))


def _combine_kernel(e_ref, r_ref, ps_ref, x_ref, w_ref, mod_ref, gain_ref, y_hbm, o_ref, ybuf, sem, *, res_row):
    tr = x_ref.shape[0]
    base = pl.program_id(0) * tr

    def copy(r, k):
        a = (base + r) * 2 + k
        src = ps_ref[e_ref[a]] + r_ref[a]
        return pltpu.make_async_copy(y_hbm.at[pl.ds(src, 1)], ybuf.at[k, pl.ds(r, 1)], sem)

    def start(r, carry):
        copy(r, 0).start()
        copy(r, 1).start()
        return carry
    lax.fori_loop(0, tr, start, 0)

    def wait(r, carry):
        copy(r, 0).wait()
        copy(r, 1).wait()
        return carry
    lax.fori_loop(0, tr, wait, 0)

    w = w_ref[...]
    y = ybuf[0] * w[:, 0:1] + ybuf[1] * w[:, 1:2]
    x = x_ref[...] + mod_ref[res_row:res_row + 1, :] * y
    o_ref[...] = _rms(x, gain_ref[...])


def _combine_final(x, ybuf, wts, mod, gain, e_flat, r_flat, pstarts, seq, res_row):
    t, d = x.shape
    tr = _tile(seq, 256, SUBLANES)
    per_b = seq // tr
    grid_spec = pltpu.PrefetchScalarGridSpec(
        num_scalar_prefetch=3, grid=(t // tr,),
        in_specs=[pl.BlockSpec((tr, d), lambda i, *_: (i, 0)), pl.BlockSpec((tr, LANES), lambda i, *_: (i, 0)),
                  pl.BlockSpec((None, SUBLANES, d), lambda i, *_: (i // per_b, 0, 0)),
                  pl.BlockSpec((1, d), lambda i, *_: (0, 0)), pl.BlockSpec(memory_space=pl.ANY)],
        out_specs=pl.BlockSpec((tr, d), lambda i, *_: (i, 0)),
        scratch_shapes=[pltpu.VMEM((2, tr, d), F32), pltpu.SemaphoreType.DMA(())])
    return pl.pallas_call(
        functools.partial(_combine_kernel, res_row=res_row), grid_spec=grid_spec,
        out_shape=jax.ShapeDtypeStruct((t, d), F32), compiler_params=_cparams(1),
    )(e_flat, r_flat, pstarts, x, wts, mod, gain.reshape(1, d), ybuf)


def _split_w_in(w_in, d):
    sizes = (DIL_WIDTH, DIL_WIDTH, DIL_WIDTH, MLA_Q_LORA, MLA_KV_LORA, MLA_ROPE,
             MOBA_HEADS * HEAD_DIM, MOBA_HEADS * HEAD_DIM, MOBA_HEADS * HEAD_DIM,
             DSA_HEADS * HEAD_DIM, HEAD_DIM, HEAD_DIM, IDX_HEADS * IDX_DIM, IDX_DIM, IDX_HEADS, N_BRANCH * d)
    names = ("a_q", "a_k", "a_v", "b_cq", "b_ckv", "b_kr", "c_q", "c_k", "c_v",
             "d_q", "d_k", "d_v", "d_iq", "d_ik", "d_iw", "g")
    assert w_in.shape[1] == sum(sizes)
    parts, o = {}, 0
    for nme, sz in zip(names, sizes):
        parts[nme] = w_in[:, o:o + sz]
        o += sz
    return parts


def _cat_cols(parts, names, pad_to=None):
    cols, offs, o = [], {}, 0
    for nme in names:
        wpart = parts[nme]
        width = -(-wpart.shape[1] // LANES) * LANES
        if width != wpart.shape[1]:
            wpart = jnp.pad(wpart, ((0, 0), (0, width - wpart.shape[1])))
        cols.append(wpart.astype(BF16))
        offs[nme] = o // LANES
        o += width
    if pad_to is not None and o % pad_to:
        cols.append(jnp.zeros((cols[0].shape[0], pad_to - o % pad_to), BF16))
    return jnp.concatenate(cols, axis=1), offs


def _mixer(h, x, mod, bsz, seq, tables, w_in, q_norm, q_up, kv_norm, kv_up, w_branch, w_out):
    t, d = h.shape
    rope_h, rope_iq, rope_ik, rope_mla = tables
    parts = _split_w_in(w_in, d)
    w_rot, ro = _cat_cols(parts, ("a_q", "a_k", "c_q", "c_k", "d_q", "d_k"))
    w_plain, po = _cat_cols(parts, ("b_cq", "a_v", "c_v", "b_ckv", "d_v", "b_kr"), pad_to=1024)

    rot = _matmul(h, w_rot, out_dtype=BF16, tn_pref=1024, epi="rope", rope=rope_h, half=ROT_DIM // 2)
    plain = _matmul(h, w_plain, out_dtype=BF16, tn_pref=1024)
    iq = _matmul(h, parts["d_iq"].astype(BF16), out_dtype=BF16, tn_pref=1024, epi="rope", rope=rope_iq, half=IDX_ROT // 2)
    ik = _matmul(h, _cat_cols(parts, ("d_ik",))[0], out_dtype=F32, tn_pref=LANES, epi="rope", rope=rope_ik, half=IDX_ROT // 2)
    iw = _matmul(h, _cat_cols(parts, ("d_iw",))[0], out_dtype=F32, tn_pref=LANES)
    gates = _matmul(h, parts["g"].astype(BF16), out_dtype=BF16, tn_pref=1024, epi="sigmoid")

    o_a = _dilated_mixture(rot, plain, bsz, seq, ro["a_q"], ro["a_k"], po["a_v"])

    hq = MLA_NOPE + MLA_ROPE
    wq = q_up.reshape(MLA_Q_LORA, MLA_HEADS, hq)
    wq = jnp.pad(wq, ((0, 0), (0, 0), (0, MLA_QK_PAD - hq))).reshape(MLA_Q_LORA, MLA_HEADS * MLA_QK_PAD).astype(BF16)
    wkv = kv_up.reshape(MLA_KV_LORA, MLA_HEADS, MLA_NOPE + MLA_V)
    wkv = jnp.concatenate([wkv[:, :, :MLA_NOPE].reshape(MLA_KV_LORA, -1), wkv[:, :, MLA_NOPE:].reshape(MLA_KV_LORA, -1)],
                          axis=1).astype(BF16)
    q_cat, k_cat, v_b = _mla_project(plain, po["b_cq"], po["b_ckv"], po["b_kr"], q_norm, wq, kv_norm, wkv, rope_mla)
    o_b = _flash(q_cat, k_cat, v_b, bsz, seq, MLA_HEADS, MLA_QK_PAD, MLA_V, 0, 0, 0, hq ** -0.5)

    kmean = _moba_kmean(rot, bsz, seq, ro["c_k"])
    o_c = _flash(rot, rot, plain, bsz, seq, MOBA_HEADS, HEAD_DIM, HEAD_DIM, ro["c_q"], ro["c_k"], po["c_v"],
                 HEAD_DIM ** -0.5, kmean=kmean)

    o_d = _dsa_attention(rot, plain, iq, ik, iw, bsz, seq, ro["d_q"], ro["d_k"], po["d_v"])

    sizes = (DIL_OUT, MLA_HEADS * MLA_V, MOBA_HEADS * HEAD_DIM, DSA_HEADS * HEAD_DIM)
    projs, o = [], 0
    for sz in sizes:
        projs.append(w_branch[o:o + sz].astype(BF16))
        o += sz
    y = _branch_merge((o_a, o_b, o_c, o_d), projs, gates, d)
    return _matmul(y, w_out.astype(BF16), out_dtype=F32, tn_pref=512, epi="resid", xres=x, mod=mod, res_row=2, seq=seq)


def kernel(x, c, positions, w_ada, b_ada, ada_table_0, mix_norm_0, w_in_0, mla_q_norm_0, mla_q_up_0, mla_kv_norm_0, mla_kv_up_0, w_branch_0, w_out_0, ffn_norm_0, ffn_gate_0, ffn_up_0, ffn_down_0, ada_table_1, mix_norm_1, w_in_1, mla_q_norm_1, mla_q_up_1, mla_kv_norm_1, mla_kv_up_1, w_branch_1, w_out_1, ffn_norm_1, router_1, expert_gate_1, expert_up_1, expert_down_1, final_norm):
    bsz, seq, d = x.shape
    t = bsz * seq
    xf = x.reshape(t, d)
    mod0, mod1 = _ada_mod(c, w_ada, b_ada, ada_table_0, ada_table_1)

    pos_b = jnp.broadcast_to(positions.reshape(t, 1).astype(F32), (t, LANES))
    tables = (_rope_tables(pos_b, _rope_pattern(HEAD_DIM, ROT_DIM, LANES)),
              _rope_tables(pos_b, _rope_pattern(IDX_DIM, IDX_ROT, LANES)),
              _rope_tables(pos_b, _rope_pattern(IDX_DIM, IDX_ROT, IDX_DIM)),
              _rope_tables(pos_b, _rope_pattern(MLA_ROPE, MLA_ROPE, MLA_ROPE)))

    (h,) = _norm_mod(xf, mod0, mix_norm_0, seq, shift_row=0, scale_row=1)
    xf = _mixer(h, xf, mod0, bsz, seq, tables, w_in_0, mla_q_norm_0, mla_q_up_0, mla_kv_norm_0, mla_kv_up_0, w_branch_0, w_out_0)
    (h,) = _norm_mod(xf, mod0, ffn_norm_0, seq, shift_row=3, scale_row=4)
    y = _ffn_dense(h, ffn_gate_0.astype(BF16), ffn_up_0.astype(BF16), ffn_down_0.astype(BF16))

    xf, h = _norm_mod(xf, mod1, mix_norm_1, seq, shift_row=0, scale_row=1, y=y, res_mod=mod0, res_row=5)
    xf = _mixer(h, xf, mod1, bsz, seq, tables, w_in_1, mla_q_norm_1, mla_q_up_1, mla_kv_norm_1, mla_kv_up_1, w_branch_1, w_out_1)
    w_r = jnp.pad(router_1, ((0, 0), (0, LANES - N_EXPERTS))).astype(BF16)
    h, logits = _norm_mod(xf, mod1, ffn_norm_1, seq, shift_row=3, scale_row=4, w_router=w_r, h_dtype=F32)
    info, wts, cnt = _route(logits)
    e_flat = info[:, 0:2].reshape(2 * t)
    r_flat = info[:, 2:4].reshape(2 * t)
    counts = cnt[0, :N_EXPERTS].astype(I32)
    rows = min(MOE_ROWS, t)
    padded = (counts + rows - 1) // rows * rows
    pends = jnp.cumsum(padded).astype(I32)
    pstarts = pends - padded
    p_rows = (-(-2 * t // rows) + N_EXPERTS) * rows
    xbuf = _dispatch(h, e_flat, r_flat, pstarts, p_rows)
    ybuf = _ffn_experts(xbuf, pends, expert_gate_1.astype(BF16), expert_up_1.astype(BF16), expert_down_1.astype(BF16), rows)
    out = _combine_final(xf, ybuf, wts, mod1, final_norm, e_flat, r_flat, pstarts, seq, res_row=5)
    return out.reshape(bsz, seq, d)
```

```python
import functools

import numpy as np
import jax
import jax.numpy as jnp
from jax import lax
from jax.experimental import pallas as pl
from jax.experimental.pallas import tpu as pltpu

F32, BF16, I32, U32 = jnp.float32, jnp.bfloat16, jnp.int32, jnp.uint32

LANES = 128
SUBLANES = 8
VMEM_LIMIT_BYTES = 56 * 1024 * 1024

HEAD_DIM = 128
ROT_DIM = HEAD_DIM // 4
ROPE_THETA = 500000.0
NORM_EPS = 1e-6
MASK_VALUE = -1e30
DIL_GROUPS = ((128, 1), (512, 4), (2048, 16))
DIL_HEADS_PER_GROUP = 4
DIL_HEADS = len(DIL_GROUPS) * DIL_HEADS_PER_GROUP
DIL_WIDTH = DIL_HEADS * HEAD_DIM
DIL_OUT = DIL_HEADS_PER_GROUP * HEAD_DIM
MLA_HEADS = 8
MLA_Q_LORA = 1536
MLA_KV_LORA = 512
MLA_NOPE = 128
MLA_ROPE = 64
MLA_V = 128
MLA_QK_PAD = 256
MOBA_HEADS = 8
MOBA_BLOCK = 256
MOBA_TOPK = 3
DSA_HEADS = 8
DSA_TOPK = 256
IDX_HEADS = 32
IDX_DIM = 64
IDX_ROT = IDX_DIM // 4
N_BRANCH = 4
N_EXPERTS = 8
N_MOD = 6
INT_MIN = -2147483648

DSA_TQ = 256
DSA_TK = 512
MOE_ROWS = 512


def _cparams(n_axes):
    return pltpu.CompilerParams(dimension_semantics=("arbitrary",) * n_axes,
                                vmem_limit_bytes=VMEM_LIMIT_BYTES)


def _tile(n, pref, mult=LANES):
    if n <= pref:
        return n
    t = (pref // mult) * mult
    while t > mult and n % t:
        t -= mult
    assert n % t == 0, (n, pref, mult)
    return t


def _nt_dot(a, b):
    return lax.dot_general(a, b, (((1,), (1,)), ((), ())), preferred_element_type=F32)


def _pack_bf16_pairs(x):
    n = x.shape[1]
    bits = lax.bitcast_convert_type(x.astype(BF16).astype(F32), U32)
    return (bits[:, n // 2:] & jnp.uint32(0xFFFF0000)) | (bits[:, :n // 2] >> 16)


def _unpack_bf16_pairs(w):
    lo = lax.bitcast_convert_type(w << 16, F32).astype(BF16)
    hi = lax.bitcast_convert_type(w & jnp.uint32(0xFFFF0000), F32).astype(BF16)
    return lo, hi


def _rope_pattern(group, rot, active):
    half = rot // 2
    freqs = ROPE_THETA ** (-jnp.arange(half, dtype=F32) * (2.0 / rot))
    lane = np.arange(LANES)
    p = lane % group
    on = (lane < active) & (p < rot)
    idx = np.where(on, p % half, 0)
    freq = jnp.where(jnp.asarray(on), freqs[idx], 0.0)
    first = (on & (p < half)).astype(np.float32)
    second = (on & (p >= half)).astype(np.float32)
    pat = jnp.zeros((SUBLANES, LANES), F32)
    return pat.at[0].set(freq).at[1].set(jnp.asarray(first)).at[2].set(jnp.asarray(second))


def _rope_table_kernel(pos_ref, pat_ref, c_ref, sa_ref, sb_ref):
    ang = pos_ref[...] * pat_ref[0:1, :]
    c_ref[...] = jnp.cos(ang)
    sin = jnp.sin(ang)
    sa_ref[...] = -sin * pat_ref[1:2, :]
    sb_ref[...] = sin * pat_ref[2:3, :]


def _rope_tables(pos_b, pat):
    t = pos_b.shape[0]
    ts = _tile(t, 1024)
    spec = pl.BlockSpec((ts, LANES), lambda i: (i, 0))
    sds = jax.ShapeDtypeStruct((t, LANES), F32)
    return pl.pallas_call(
        _rope_table_kernel, grid=(t // ts,),
        in_specs=[spec, pl.BlockSpec((SUBLANES, LANES), lambda i: (0, 0))],
        out_specs=[spec, spec, spec], out_shape=[sds, sds, sds],
        compiler_params=_cparams(1))(pos_b, pat)


def _apply_rope(x, c, sa, sb, half):
    return x * c + pltpu.roll(x, LANES - half, 1) * sa + pltpu.roll(x, half, 1) * sb


def _ada_kernel(c_ref, w_ref, b_ref, t0_ref, t1_ref, o0_ref, o1_ref):
    c = c_ref[...]
    a = (c * jax.nn.sigmoid(c)).astype(BF16)
    acc = jnp.dot(a, w_ref[...].astype(BF16), preferred_element_type=F32) + b_ref[...]
    o0_ref[...] = acc + t0_ref[...]
    o1_ref[...] = acc + t1_ref[...]


def _ada_mod(c, w_ada, b_ada, table0, table1):
    b, d = c.shape
    n = w_ada.shape[1]
    bp = -(-b // SUBLANES) * SUBLANES
    cp = jnp.zeros((bp, d), F32).at[:b].set(c)
    tn = _tile(n, 512)
    row = pl.BlockSpec((1, tn), lambda j: (0, j))
    out = pl.BlockSpec((bp, tn), lambda j: (0, j))
    sds = jax.ShapeDtypeStruct((bp, n), F32)
    m0, m1 = pl.pallas_call(
        _ada_kernel, grid=(n // tn,),
        in_specs=[pl.BlockSpec((bp, d), lambda j: (0, 0)), pl.BlockSpec((d, tn), lambda j: (0, j)), row, row, row],
        out_specs=[out, out], out_shape=[sds, sds], compiler_params=_cparams(1),
    )(cp, w_ada, b_ada.reshape(1, n), table0.reshape(1, n), table1.reshape(1, n))

    def pack(m):
        m = m[:b].reshape(b, N_MOD, d)
        return jnp.concatenate([m, jnp.zeros((b, SUBLANES - N_MOD, d), F32)], axis=1)
    return pack(m0), pack(m1)


def _norm_kernel(*refs, res_row, shift_row, scale_row, with_logits, h_dtype):
    it = iter(refs)
    x_ref = next(it)
    y_ref, rmod_ref = (next(it), next(it)) if res_row is not None else (None, None)
    mod_ref, gain_ref = next(it), next(it)
    wr_ref = next(it) if with_logits else None
    xo_ref = next(it) if res_row is not None else None
    h_ref = next(it)
    lg_ref = next(it) if with_logits else None

    x = x_ref[...]
    if res_row is not None:
        x = x + rmod_ref[res_row:res_row + 1, :] * y_ref[...].astype(F32)
        xo_ref[...] = x
    y = x * lax.rsqrt(jnp.mean(x * x, axis=-1, keepdims=True) + NORM_EPS) * gain_ref[...]
    h = y * (1.0 + mod_ref[scale_row:scale_row + 1, :]) + mod_ref[shift_row:shift_row + 1, :]
    if h_dtype == U32:
        h_ref[...] = _pack_bf16_pairs(h)
    else:
        h_ref[...] = h.astype(h_dtype)
    if with_logits:
        lg_ref[...] = jnp.dot(h.astype(BF16), wr_ref[...], preferred_element_type=F32)


def _norm_mod(x, mod, gain, seq, *, shift_row, scale_row, y=None, res_mod=None, res_row=None, w_router=None,
              h_dtype=BF16):
    t, d = x.shape
    ts = _tile(seq, 256, SUBLANES)
    per_b = seq // ts
    blk = pl.BlockSpec((ts, d), lambda i: (i, 0))
    mod_spec = pl.BlockSpec((None, SUBLANES, d), lambda i: (i // per_b, 0, 0))
    in_specs, args = [blk], [x]
    if res_row is not None:
        in_specs += [blk, mod_spec]
        args += [y, res_mod]
    in_specs += [mod_spec, pl.BlockSpec((1, d), lambda i: (0, 0))]
    args += [mod, gain.reshape(1, d)]
    if w_router is not None:
        in_specs.append(pl.BlockSpec((d, LANES), lambda i: (0, 0)))
        args.append(w_router)
    out_specs, out_shape = [], []
    if res_row is not None:
        out_specs.append(blk)
        out_shape.append(jax.ShapeDtypeStruct((t, d), F32))
    hd = d // 2 if h_dtype == U32 else d
    out_specs.append(pl.BlockSpec((ts, hd), lambda i: (i, 0)))
    out_shape.append(jax.ShapeDtypeStruct((t, hd), h_dtype))
    if w_router is not None:
        out_specs.append(pl.BlockSpec((ts, LANES), lambda i: (i, 0)))
        out_shape.append(jax.ShapeDtypeStruct((t, LANES), F32))
    kern = functools.partial(_norm_kernel, res_row=res_row, shift_row=shift_row, scale_row=scale_row,
                             with_logits=w_router is not None, h_dtype=h_dtype)
    return pl.pallas_call(kern, grid=(t // ts,), in_specs=in_specs, out_specs=out_specs, out_shape=out_shape,
                          compiler_params=_cparams(1))(*args)


def _mm_kernel(*refs, epi, half, res_row):
    a_ref, w_ref = refs[0], refs[1]
    o_ref = refs[-1]
    acc = jnp.dot(a_ref[...], w_ref[...], preferred_element_type=F32)
    if epi == "plain":
        o_ref[...] = acc.astype(o_ref.dtype)
    elif epi == "sigmoid":
        o_ref[...] = jax.nn.sigmoid(acc).astype(o_ref.dtype)
    elif epi == "rope":
        c, sa, sb = refs[2][...], refs[3][...], refs[4][...]
        for ch in range(acc.shape[1] // LANES):
            sl = slice(ch * LANES, (ch + 1) * LANES)
            o_ref[:, sl] = _apply_rope(acc[:, sl], c, sa, sb, half).astype(o_ref.dtype)
    elif epi == "resid":
        xres_ref, mod_ref = refs[2], refs[3]
        o_ref[...] = xres_ref[...] + mod_ref[res_row:res_row + 1, :] * acc
    else:
        raise ValueError(epi)


def _matmul(a, w, *, out_dtype, tn_pref, epi="plain", rope=None, half=0, xres=None, mod=None, res_row=None, seq=None):
    m, k = a.shape
    n = w.shape[1]
    tm = _tile(seq if seq is not None else m, 1024, SUBLANES)
    tn = _tile(n, tn_pref)
    in_specs = [pl.BlockSpec((tm, k), lambda i, j: (i, 0)), pl.BlockSpec((k, tn), lambda i, j: (0, j))]
    args = [a, w]
    if epi == "rope":
        in_specs += [pl.BlockSpec((tm, LANES), lambda i, j: (i, 0))] * 3
        args += list(rope)
    if epi == "resid":
        per_b = seq // tm
        in_specs += [pl.BlockSpec((tm, tn), lambda i, j: (i, j)),
                     pl.BlockSpec((None, SUBLANES, tn), lambda i, j: (i // per_b, 0, j))]
        args += [xres, mod]
    kern = functools.partial(_mm_kernel, epi=epi, half=half, res_row=res_row)
    return pl.pallas_call(
        kern, grid=(m // tm, n // tn), in_specs=in_specs,
        out_specs=pl.BlockSpec((tm, tn), lambda i, j: (i, j)),
        out_shape=jax.ShapeDtypeStruct((m, n), out_dtype), compiler_params=_cparams(2))(*args)


def _dil_kernel(q_ref, kp_ref, ko_ref, vp_ref, vo_ref, o_ref, lse_ref, *, scale):
    i = pl.program_id(2)
    w = q_ref.shape[0]
    row = lax.broadcasted_iota(I32, (w, w), 0)
    col = lax.broadcasted_iota(I32, (w, w), 1)
    for h in range(DIL_HEADS_PER_GROUP):
        sl = slice(h * HEAD_DIM, (h + 1) * HEAD_DIM)
        q = q_ref[:, sl]
        s_o = jnp.where(col <= row, _nt_dot(q, ko_ref[:, sl]) * scale, MASK_VALUE)
        s_p = jnp.where(col >= row, _nt_dot(q, kp_ref[:, sl]) * scale, MASK_VALUE)
        s_p = jnp.where(i > 0, s_p, MASK_VALUE)
        m = jnp.maximum(jnp.max(s_o, axis=1, keepdims=True), jnp.max(s_p, axis=1, keepdims=True))
        p_o = jnp.exp(s_o - m)
        p_p = jnp.exp(s_p - m)
        l = jnp.sum(p_o, axis=1, keepdims=True) + jnp.sum(p_p, axis=1, keepdims=True)
        o = (jnp.dot(p_o.astype(BF16), vo_ref[:, sl], preferred_element_type=F32)
             + jnp.dot(p_p.astype(BF16), vp_ref[:, sl], preferred_element_type=F32))
        o_ref[:, sl] = o / l
        lse_ref[:, sl] = jnp.broadcast_to(m + jnp.log(l), o.shape)


def _dilated_group(qk, v, bsz, seq, g):
    window, dil = DIL_GROUPS[g]
    wsub = window // dil
    n = seq // dil
    assert wsub == HEAD_DIM and n % wsub == 0
    nb = n // wsub
    qk_v = qk.reshape(bsz, n, dil * 2 * DIL_OUT)
    v_v = v.reshape(bsz, n, dil * DIL_OUT)
    blk = (None, wsub, DIL_OUT)
    q_spec = pl.BlockSpec(blk, lambda b, r, i: (b, i, 2 * r))
    k_own = pl.BlockSpec(blk, lambda b, r, i: (b, i, 2 * r + 1))
    k_prev = pl.BlockSpec(blk, lambda b, r, i: (b, jnp.maximum(i - 1, 0), 2 * r + 1))
    v_own = pl.BlockSpec(blk, lambda b, r, i: (b, i, r))
    v_prev = pl.BlockSpec(blk, lambda b, r, i: (b, jnp.maximum(i - 1, 0), r))
    sds = jax.ShapeDtypeStruct((bsz, n, dil * DIL_OUT), F32)
    o, lse = pl.pallas_call(
        functools.partial(_dil_kernel, scale=HEAD_DIM ** -0.5), grid=(bsz, dil, nb),
        in_specs=[q_spec, k_prev, k_own, v_prev, v_own],
        out_specs=[v_own, v_own], out_shape=[sds, sds], compiler_params=_cparams(3),
    )(qk_v, qk_v, qk_v, v_v, v_v)
    return o.reshape(bsz * seq, DIL_OUT), lse.reshape(bsz * seq, DIL_OUT)


def _dil_combine_kernel(o0, o1, o2, l0, l1, l2, out_ref):
    a, b, c = l0[...], l1[...], l2[...]
    m = jnp.maximum(jnp.maximum(a, b), c)
    ea, eb, ec = jnp.exp(a - m), jnp.exp(b - m), jnp.exp(c - m)
    out_ref[...] = ((ea * o0[...] + eb * o1[...] + ec * o2[...]) / (ea + eb + ec)).astype(out_ref.dtype)


def _dilated_mixture(qks, vs, bsz, seq):
    outs, lses = zip(*[_dilated_group(qks[g], vs[g], bsz, seq, g) for g in range(len(DIL_GROUPS))])
    t = bsz * seq
    ts = _tile(t, 1024, SUBLANES)
    spec = pl.BlockSpec((ts, DIL_OUT), lambda i: (i, 0))
    return pl.pallas_call(
        _dil_combine_kernel, grid=(t // ts,), in_specs=[spec] * 6, out_specs=spec,
        out_shape=jax.ShapeDtypeStruct((t, DIL_OUT), BF16), compiler_params=_cparams(1))(*outs, *lses)


def _rms(x, gain):
    return x * lax.rsqrt(jnp.mean(x * x, axis=-1, keepdims=True) + NORM_EPS) * gain


def _mla_q_kernel(cq_ref, gain_ref, w_ref, c_ref, sa_ref, sb_ref, o_ref):
    cqn = _rms(cq_ref[...].astype(F32), gain_ref[...]).astype(BF16)
    acc = jnp.dot(cqn, w_ref[...], preferred_element_type=F32)
    c, sa, sb = c_ref[...], sa_ref[...], sb_ref[...]
    for ch in range(acc.shape[1] // LANES):
        sl = slice(ch * LANES, (ch + 1) * LANES)
        x = acc[:, sl]
        if ch % 2 == 1:
            x = _apply_rope(x, c, sa, sb, MLA_ROPE // 2)
        o_ref[:, sl] = x.astype(o_ref.dtype)


def _mla_kv_kernel(ckv_ref, gain_ref, w_ref, kr_ref, c_ref, sa_ref, sb_ref, k_ref, v_ref):
    ckvn = _rms(ckv_ref[...].astype(F32), gain_ref[...]).astype(BF16)
    acc = jnp.dot(ckvn, w_ref[...], preferred_element_type=F32)
    kpe = _apply_rope(kr_ref[...].astype(F32), c_ref[...], sa_ref[...], sb_ref[...], MLA_ROPE // 2).astype(k_ref.dtype)
    hw = MLA_HEADS * MLA_NOPE
    for h in range(MLA_HEADS):
        k_ref[:, h * MLA_QK_PAD:h * MLA_QK_PAD + MLA_NOPE] = acc[:, h * MLA_NOPE:(h + 1) * MLA_NOPE].astype(k_ref.dtype)
        k_ref[:, h * MLA_QK_PAD + MLA_NOPE:(h + 1) * MLA_QK_PAD] = kpe
    v_ref[...] = acc[:, hw:].astype(v_ref.dtype)


def _mla_project(plain, cq_col, ckv_col, kr_col, q_gain, wq, kv_gain, wkv, rope):
    t = plain.shape[0]
    ts = _tile(t, 512, SUBLANES)
    tab = pl.BlockSpec((ts, LANES), lambda i: (i, 0))
    nq = MLA_HEADS * MLA_QK_PAD
    q_cat = pl.pallas_call(
        _mla_q_kernel, grid=(t // ts,),
        in_specs=[pl.BlockSpec((ts, MLA_Q_LORA), lambda i: (i, cq_col * LANES // MLA_Q_LORA)),
                  pl.BlockSpec((1, MLA_Q_LORA), lambda i: (0, 0)),
                  pl.BlockSpec((MLA_Q_LORA, nq), lambda i: (0, 0)), tab, tab, tab],
        out_specs=pl.BlockSpec((ts, nq), lambda i: (i, 0)),
        out_shape=jax.ShapeDtypeStruct((t, nq), BF16), compiler_params=_cparams(1),
    )(plain, q_gain.reshape(1, -1), wq, *rope)
    nv = MLA_HEADS * MLA_V
    k_cat, v = pl.pallas_call(
        _mla_kv_kernel, grid=(t // ts,),
        in_specs=[pl.BlockSpec((ts, MLA_KV_LORA), lambda i: (i, ckv_col * LANES // MLA_KV_LORA)),
                  pl.BlockSpec((1, MLA_KV_LORA), lambda i: (0, 0)),
                  pl.BlockSpec((MLA_KV_LORA, MLA_HEADS * MLA_NOPE + nv), lambda i: (0, 0)),
                  pl.BlockSpec((ts, LANES), lambda i: (i, kr_col)), tab, tab, tab],
        out_specs=[pl.BlockSpec((ts, nq), lambda i: (i, 0)), pl.BlockSpec((ts, nv), lambda i: (i, 0))],
        out_shape=[jax.ShapeDtypeStruct((t, nq), BF16), jax.ShapeDtypeStruct((t, nv), BF16)],
        compiler_params=_cparams(1),
    )(plain, kv_gain.reshape(1, -1), wkv, plain, *rope)
    return q_cat, k_cat, v


def _flash_kernel(*refs, scale, moba):
    if moba:
        q_ref, k_ref, v_ref, km_ref, o_ref, m_sc, l_sc, acc_sc, sel_sc = refs
    else:
        q_ref, k_ref, v_ref, o_ref, m_sc, l_sc, acc_sc = refs
    qi, step = pl.program_id(2), pl.program_id(3)
    kv = qi - step
    tq, tk = q_ref.shape[0], k_ref.shape[0]

    @pl.when(step == 0)
    def _init():
        m_sc[...] = jnp.full(m_sc.shape, MASK_VALUE, F32)
        l_sc[...] = jnp.zeros(l_sc.shape, F32)
        acc_sc[...] = jnp.zeros(acc_sc.shape, F32)
        if moba:
            lane = lax.broadcasted_iota(I32, (tq, LANES), 1)
            own = (qi * tq + lax.broadcasted_iota(I32, (tq, LANES), 0)) // MOBA_BLOCK
            valid = lane < own
            g = jnp.where(valid, _nt_dot(q_ref[...], km_ref[...]), MASK_VALUE)
            sel = jnp.zeros((tq, LANES), F32)
            for _ in range(MOBA_TOPK):
                mx = jnp.max(g, axis=1, keepdims=True)
                first = jnp.min(jnp.where(g == mx, lane, LANES), axis=1, keepdims=True)
                pick = lane == first
                sel = jnp.where(pick, 1.0, sel)
                g = jnp.where(pick, -jnp.inf, g)
            sel_sc[...] = jnp.where(valid, sel, 0.0)

    def block_flags():
        lane = lax.broadcasted_iota(I32, (tq, LANES), 1)
        sel = sel_sc[...]
        nsub = tk // MOBA_BLOCK
        flags = []
        for c in range(nsub):
            f = jnp.sum(jnp.where(lane == kv * nsub + c, sel, 0.0), axis=1, keepdims=True)
            flags.append(jnp.broadcast_to(f, (tq, MOBA_BLOCK)))
        return flags[0] if nsub == 1 else jnp.concatenate(flags, axis=1)

    def update(s):
        m_prev = m_sc[:, 0:1]
        m_new = jnp.maximum(m_prev, jnp.max(s, axis=1, keepdims=True))
        alpha = jnp.exp(m_prev - m_new)
        p = jnp.exp(s - m_new)
        l_sc[...] = jnp.broadcast_to(alpha * l_sc[:, 0:1] + jnp.sum(p, axis=1, keepdims=True), l_sc.shape)
        acc_sc[...] = alpha * acc_sc[...] + jnp.dot(p.astype(BF16), v_ref[...], preferred_element_type=F32)
        m_sc[...] = jnp.broadcast_to(m_new, m_sc.shape)

    @pl.when(step == 0)
    def _diagonal():
        s = _nt_dot(q_ref[...], k_ref[...]) * scale
        row = lax.broadcasted_iota(I32, (tq, tk), 0)
        col = lax.broadcasted_iota(I32, (tq, tk), 1)
        ok = col <= row
        if moba:
            same = (col // MOBA_BLOCK) == (row // MOBA_BLOCK)
            ok = jnp.logical_and(ok, jnp.logical_or(same, block_flags() > 0.0))
        update(jnp.where(ok, s, MASK_VALUE))

    @pl.when(jnp.logical_and(step > 0, step <= qi))
    def _past():
        s = _nt_dot(q_ref[...], k_ref[...]) * scale
        if moba:
            s = jnp.where(block_flags() > 0.0, s, MASK_VALUE)
        update(s)

    @pl.when(step == pl.num_programs(3) - 1)
    def _fin():
        o_ref[...] = (acc_sc[...] / l_sc[:, 0:1]).astype(o_ref.dtype)


def _flash(q, k, v, bsz, seq, heads, dq, dv, q_col, k_col, v_col, scale, kmean=None):
    t = bsz * seq
    tq = _tile(seq, 1024, MOBA_BLOCK)
    nq = seq // tq
    in_specs = [pl.BlockSpec((tq, dq), lambda b, h, i, s: (b * nq + i, q_col + h)),
                pl.BlockSpec((tq, dq), lambda b, h, i, s: (b * nq + jnp.maximum(i - s, 0), k_col + h)),
                pl.BlockSpec((tq, dv), lambda b, h, i, s: (b * nq + jnp.maximum(i - s, 0), v_col + h))]
    args = [q, k, v]
    scratch = [pltpu.VMEM((tq, LANES), F32), pltpu.VMEM((tq, LANES), F32), pltpu.VMEM((tq, dv), F32)]
    if kmean is not None:
        in_specs.append(pl.BlockSpec((None, None, LANES, HEAD_DIM), lambda b, h, i, s: (b, h, 0, 0)))
        args.append(kmean)
        scratch.append(pltpu.VMEM((tq, LANES), F32))
    return pl.pallas_call(
        functools.partial(_flash_kernel, scale=scale, moba=kmean is not None),
        grid=(bsz, heads, nq, nq), in_specs=in_specs,
        out_specs=pl.BlockSpec((tq, dv), lambda b, h, i, s: (b * nq + i, h)),
        out_shape=jax.ShapeDtypeStruct((t, heads * dv), BF16),
        scratch_shapes=scratch, compiler_params=_cparams(4))(*args)


def _kmean_kernel(k_ref, o_ref):
    s = k_ref.shape[0]
    nb = s // MOBA_BLOCK
    k = k_ref[...].astype(F32).reshape(nb, MOBA_BLOCK, HEAD_DIM)
    mean = jnp.sum(k, axis=1) * (1.0 / MOBA_BLOCK)
    full = jnp.concatenate([mean, jnp.zeros((LANES - nb, HEAD_DIM), F32)], axis=0)
    o_ref[...] = full.astype(o_ref.dtype)


def _moba_kmean(rot, bsz, seq, k_col):
    assert seq % MOBA_BLOCK == 0 and seq // MOBA_BLOCK <= LANES
    return pl.pallas_call(
        _kmean_kernel, grid=(bsz, MOBA_HEADS),
        in_specs=[pl.BlockSpec((seq, HEAD_DIM), lambda b, h: (b, k_col + h))],
        out_specs=pl.BlockSpec((None, None, LANES, HEAD_DIM), lambda b, h: (b, h, 0, 0)),
        out_shape=jax.ShapeDtypeStruct((bsz, MOBA_HEADS, LANES, HEAD_DIM), BF16),
        compiler_params=_cparams(2))(rot)


def _dsa_kernel(q_ref, iq_ref, iw_ref, ikbd_ref, k_ref, v_ref, o_ref, key_sc, qs_sc, m_sc, l_sc, acc_sc, *, n_keep, scale,
                wscale):
    c = pl.program_id(1)
    tq, tk = DSA_TQ, DSA_TK
    nkt = (c * tq) // tk + 1
    rowg = c * tq + lax.broadcasted_iota(I32, (tq, tk), 0)
    col = lax.broadcasted_iota(I32, (tq, tk), 1)
    w = iw_ref[...] * wscale

    def score_tile(kt, carry):
        ikb = ikbd_ref[kt]
        sc = jnp.zeros((tq, tk), F32)
        for j in range(IDX_HEADS // 2):
            rel = jnp.maximum(_nt_dot(iq_ref[:, j * LANES:(j + 1) * LANES], ikb), 0.0)
            sc = sc + rel[:, :tk] * w[:, 2 * j:2 * j + 1] + rel[:, tk:] * w[:, 2 * j + 1:2 * j + 2]
        bits = lax.bitcast_convert_type(sc, I32)
        key = bits ^ ((bits >> 31) & 0x7FFFFFFF)
        key_sc[kt] = jnp.where(kt * tk + col <= rowg, key, INT_MIN)
        return carry
    lax.fori_loop(0, nkt, score_tile, 0)

    def count_ge(cand):
        def body(kt, acc):
            ge = jnp.where(key_sc[kt] >= cand, 1.0, 0.0)
            for j in range(tk // LANES):
                acc = acc + ge[:, j * LANES:(j + 1) * LANES]
            return acc
        acc = lax.fori_loop(0, nkt, body, jnp.zeros((tq, LANES), F32))
        return jnp.sum(acc, axis=1, keepdims=True)

    keep = jnp.float32(n_keep)
    thr = jnp.where(count_ge(jnp.zeros((tq, 1), I32)) >= keep, 0, INT_MIN).astype(I32)

    def search(it, thr):
        cand = thr | (jnp.int32(1) << (30 - it))
        return jnp.where(count_ge(cand) >= keep, cand, thr)
    thr = lax.fori_loop(0, 31, search, thr)
    thr = jnp.maximum(thr, INT_MIN + 1)

    for h in range(DSA_HEADS):
        qs_sc[h * tq:(h + 1) * tq, :] = q_ref[:, h * HEAD_DIM:(h + 1) * HEAD_DIM]
    m_sc[...] = jnp.full(m_sc.shape, MASK_VALUE, F32)
    l_sc[...] = jnp.zeros(l_sc.shape, F32)
    acc_sc[...] = jnp.zeros(acc_sc.shape, F32)

    def attend(kt, carry):
        sel = key_sc[kt] >= thr
        ks = k_ref[pl.ds(pl.multiple_of(kt * tk, tk), tk), :]
        vs = v_ref[pl.ds(pl.multiple_of(kt * tk, tk), tk), :]
        s = _nt_dot(qs_sc[...], ks) * scale
        s = jnp.concatenate([jnp.where(sel, s[h * tq:(h + 1) * tq, :], MASK_VALUE) for h in range(DSA_HEADS)], axis=0)
        m_prev = m_sc[:, 0:1]
        m_new = jnp.maximum(m_prev, jnp.max(s, axis=1, keepdims=True))
        alpha = jnp.exp(m_prev - m_new)
        p = jnp.exp(s - m_new)
        l_sc[...] = jnp.broadcast_to(alpha * l_sc[:, 0:1] + jnp.sum(p, axis=1, keepdims=True), l_sc.shape)
        acc_sc[...] = alpha * acc_sc[...] + jnp.dot(p.astype(BF16), vs, preferred_element_type=F32)
        m_sc[...] = jnp.broadcast_to(m_new, m_sc.shape)
        return carry
    lax.fori_loop(0, nkt, attend, 0)

    for h in range(DSA_HEADS):
        rows = slice(h * tq, (h + 1) * tq)
        o_ref[:, h * HEAD_DIM:(h + 1) * HEAD_DIM] = (acc_sc[rows, :] / l_sc[rows, 0:1]).astype(o_ref.dtype)


def _dsa_attention(rot, plain, iq, ik, iw, bsz, seq, q_col, k_col, v_col):
    t = bsz * seq
    assert seq % DSA_TK == 0
    nkt = seq // DSA_TK
    nch = seq // DSA_TQ
    ika = ik.astype(BF16).reshape(bsz, nkt, DSA_TK, LANES)
    ikbd = jnp.concatenate([ika, jnp.roll(ika, IDX_DIM, axis=-1)], axis=2)
    n_keep = min(DSA_TOPK, seq // 4)
    qw = DSA_HEADS * HEAD_DIM
    iqw = IDX_HEADS * IDX_DIM
    kern = functools.partial(_dsa_kernel, n_keep=n_keep, scale=HEAD_DIM ** -0.5,
                             wscale=(IDX_HEADS ** -0.5) * (IDX_DIM ** -0.5))
    return pl.pallas_call(
        kern, grid=(bsz, nch),
        in_specs=[pl.BlockSpec((DSA_TQ, qw), lambda b, c: (b * nch + c, q_col * HEAD_DIM // qw)),
                  pl.BlockSpec((DSA_TQ, iqw), lambda b, c: (b * nch + c, 0)),
                  pl.BlockSpec((DSA_TQ, LANES), lambda b, c: (b * nch + c, 0)),
                  pl.BlockSpec((None, nkt, 2 * DSA_TK, LANES), lambda b, c: (b, 0, 0, 0)),
                  pl.BlockSpec((seq, HEAD_DIM), lambda b, c: (b, k_col)),
                  pl.BlockSpec((seq, HEAD_DIM), lambda b, c: (b, v_col))],
        out_specs=pl.BlockSpec((DSA_TQ, qw), lambda b, c: (b * nch + c, 0)),
        out_shape=jax.ShapeDtypeStruct((t, qw), BF16),
        scratch_shapes=[pltpu.VMEM((nkt, DSA_TQ, DSA_TK), I32), pltpu.VMEM((DSA_HEADS * DSA_TQ, HEAD_DIM), BF16),
                        pltpu.VMEM((DSA_HEADS * DSA_TQ, LANES), F32), pltpu.VMEM((DSA_HEADS * DSA_TQ, LANES), F32),
                        pltpu.VMEM((DSA_HEADS * DSA_TQ, HEAD_DIM), F32)],
        compiler_params=_cparams(2))(rot, iq, iw, ikbd, rot, plain)


def _branch_kernel(oa, ob, oc, od, pa, pb, pc, pd, ga, gb, gc, gd, y_ref):
    y = None
    for o, p, g in ((oa, pa, ga), (ob, pb, gb), (oc, pc, gc), (od, pd, gd)):
        term = g[...].astype(F32) * jnp.dot(o[...], p[...], preferred_element_type=F32)
        y = term if y is None else y + term
    y_ref[...] = y.astype(y_ref.dtype)


def _branch_merge(outs, projs, gates, d):
    t = outs[0].shape[0]
    tm = _tile(t, 1024, SUBLANES)
    tn = _tile(d, 512)
    nj = d // tn
    in_specs = [pl.BlockSpec((tm, o.shape[1]), lambda i, j: (i, 0)) for o in outs]
    in_specs += [pl.BlockSpec((p.shape[0], tn), lambda i, j: (0, j)) for p in projs]
    in_specs += [pl.BlockSpec((tm, tn), functools.partial(lambda i, j, k: (i, k * nj + j), k=k)) for k in range(N_BRANCH)]
    return pl.pallas_call(
        _branch_kernel, grid=(t // tm, nj), in_specs=in_specs,
        out_specs=pl.BlockSpec((tm, tn), lambda i, j: (i, j)),
        out_shape=jax.ShapeDtypeStruct((t, d), BF16), compiler_params=_cparams(2),
    )(*outs, *projs, gates, gates, gates, gates)


def _glu_kernel(*refs, n_prefetch):
    pends_ref = refs[0] if n_prefetch else None
    x_ref, wg_ref, wu_ref, o_ref = refs[n_prefetch:n_prefetch + 4]

    def compute():
        if x_ref.dtype == U32:
            xb_sc = refs[n_prefetch + 4]

            @pl.when(pl.program_id(1) == 0)
            def _():
                lo, hi = _unpack_bf16_pairs(x_ref[...])
                xb_sc[:, :lo.shape[1]] = lo
                xb_sc[:, lo.shape[1]:] = hi
            x = xb_sc[...]
        else:
            x = x_ref[...]
        g = jnp.dot(x, wg_ref[...], preferred_element_type=F32)
        u = jnp.dot(x, wu_ref[...], preferred_element_type=F32)
        o_ref[...] = (g * jax.nn.sigmoid(g) * u).astype(o_ref.dtype)

    if n_prefetch:
        used = pl.program_id(0) * x_ref.shape[0] < pends_ref[N_EXPERTS - 1]
        pl.when(used)(compute)

        @pl.when(jnp.logical_not(used))
        def _():
            o_ref[...] = jnp.zeros(o_ref.shape, o_ref.dtype)
    else:
        compute()


def _mmk_kernel(a_ref, w_ref, o_ref, acc_sc):
    k = pl.program_id(2)
    part = jnp.dot(a_ref[...], w_ref[...], preferred_element_type=F32)

    @pl.when(k == 0)
    def _():
        acc_sc[...] = part

    @pl.when(k > 0)
    def _():
        acc_sc[...] += part

    @pl.when(k == pl.num_programs(2) - 1)
    def _():
        o_ref[...] = acc_sc[...].astype(o_ref.dtype)


def _ffn_dense(h, wg, wu, wd):
    t, d = h.shape
    ff = wg.shape[1]
    tm = _tile(t, 1024, SUBLANES)
    tf = _tile(ff, 512)
    a = pl.pallas_call(
        functools.partial(_glu_kernel, n_prefetch=0), grid=(t // tm, ff // tf),
        in_specs=[pl.BlockSpec((tm, d), lambda i, f: (i, 0)), pl.BlockSpec((d, tf), lambda i, f: (0, f)),
                  pl.BlockSpec((d, tf), lambda i, f: (0, f))],
        out_specs=pl.BlockSpec((tm, tf), lambda i, f: (i, f)),
        out_shape=jax.ShapeDtypeStruct((t, ff), BF16), compiler_params=_cparams(2))(h, wg, wu)
    tn = _tile(d, 1024)
    tk = _tile(ff, 2048)
    return pl.pallas_call(
        _mmk_kernel, grid=(t // tm, d // tn, ff // tk),
        in_specs=[pl.BlockSpec((tm, tk), lambda i, j, k: (i, k)), pl.BlockSpec((tk, tn), lambda i, j, k: (k, j))],
        out_specs=pl.BlockSpec((tm, tn), lambda i, j, k: (i, j)),
        out_shape=jax.ShapeDtypeStruct((t, d), BF16),
        scratch_shapes=[pltpu.VMEM((tm, tn), F32)], compiler_params=_cparams(3))(a, wd)


def _down_experts_kernel(pends_ref, a_ref, w_ref, o_ref):
    used = pl.program_id(0) * a_ref.shape[0] < pends_ref[N_EXPERTS - 1]

    @pl.when(used)
    def _():
        o_ref[...] = jnp.dot(a_ref[...], w_ref[...], preferred_element_type=F32)

    @pl.when(jnp.logical_not(used))
    def _():
        o_ref[...] = jnp.zeros(o_ref.shape, o_ref.dtype)


def _ffn_experts(xbuf, pends, wg, wu, wd, rows):
    p = xbuf.shape[0]
    d, ff = wg.shape[1], wg.shape[2]
    tf = _tile(ff, 512)

    def expert(i, pends_ref):
        e = jnp.int32(0)
        for j in range(N_EXPERTS - 1):
            e = e + (i * rows >= pends_ref[j]).astype(I32)
        return e

    a = pl.pallas_call(
        functools.partial(_glu_kernel, n_prefetch=1),
        grid_spec=pltpu.PrefetchScalarGridSpec(
            num_scalar_prefetch=1, grid=(p // rows, ff // tf),
            in_specs=[pl.BlockSpec((rows, d // 2), lambda i, f, pe: (i, 0)),
                      pl.BlockSpec((None, d, tf), lambda i, f, pe: (expert(i, pe), 0, f)),
                      pl.BlockSpec((None, d, tf), lambda i, f, pe: (expert(i, pe), 0, f))],
            out_specs=pl.BlockSpec((rows, tf), lambda i, f, pe: (i, f)),
            scratch_shapes=[pltpu.VMEM((rows, d), BF16)]),
        out_shape=jax.ShapeDtypeStruct((p, ff), BF16), compiler_params=_cparams(2))(pends, xbuf, wg, wu)
    tn = _tile(d, 1024)
    return pl.pallas_call(
        _down_experts_kernel,
        grid_spec=pltpu.PrefetchScalarGridSpec(
            num_scalar_prefetch=1, grid=(p // rows, d // tn),
            in_specs=[pl.BlockSpec((rows, ff), lambda i, j, pe: (i, 0)),
                      pl.BlockSpec((None, ff, tn), lambda i, j, pe: (expert(i, pe), 0, j))],
            out_specs=pl.BlockSpec((rows, tn), lambda i, j, pe: (i, j))),
        out_shape=jax.ShapeDtypeStruct((p, d), F32), compiler_params=_cparams(2))(pends, a, wd)


def _route_kernel(lg_ref, info_ref, w_ref, cnt_ref, carry_sc):
    i = pl.program_id(0)

    @pl.when(i == 0)
    def _():
        carry_sc[...] = jnp.zeros(carry_sc.shape, F32)

    tr = lg_ref.shape[0]
    lane = lax.broadcasted_iota(I32, (tr, LANES), 1)
    lg = jnp.where(lane < N_EXPERTS, lg_ref[...], -jnp.inf)
    m1 = jnp.max(lg, axis=1, keepdims=True)
    i1 = jnp.min(jnp.where(lg == m1, lane, LANES), axis=1, keepdims=True)
    lg2 = jnp.where(lane == i1, -jnp.inf, lg)
    m2 = jnp.max(lg2, axis=1, keepdims=True)
    i2 = jnp.min(jnp.where(lg2 == m2, lane, LANES), axis=1, keepdims=True)
    e = jnp.exp(m2 - m1)
    w1 = 1.0 / (1.0 + e)
    w2 = e * w1
    oh = jnp.where(jnp.logical_or(lane == i1, lane == i2), 1.0, 0.0)
    tri = jnp.where(lax.broadcasted_iota(I32, (tr, tr), 1) < lax.broadcasted_iota(I32, (tr, tr), 0), 1.0, 0.0)
    before = jnp.dot(tri.astype(BF16), oh.astype(BF16), preferred_element_type=F32) + carry_sc[0:1, :]
    r1 = jnp.sum(jnp.where(lane == i1, before, 0.0), axis=1, keepdims=True).astype(I32)
    r2 = jnp.sum(jnp.where(lane == i2, before, 0.0), axis=1, keepdims=True).astype(I32)
    carry_sc[0:1, :] = carry_sc[0:1, :] + jnp.sum(oh, axis=0, keepdims=True)
    info_ref[...] = jnp.where(lane == 0, i1, jnp.where(lane == 1, i2, jnp.where(lane == 2, r1, jnp.where(lane == 3, r2, 0))))
    w_ref[...] = jnp.where(lane == 0, w1, jnp.where(lane == 1, w2, 0.0))
    cnt_ref[...] = carry_sc[...]


def _route(logits):
    t = logits.shape[0]
    tr = _tile(t, 256, SUBLANES)
    blk = pl.BlockSpec((tr, LANES), lambda i: (i, 0))
    return pl.pallas_call(
        _route_kernel, grid=(t // tr,), in_specs=[blk],
        out_specs=[blk, blk, pl.BlockSpec((SUBLANES, LANES), lambda i: (0, 0))],
        out_shape=[jax.ShapeDtypeStruct((t, LANES), I32), jax.ShapeDtypeStruct((t, LANES), F32),
                   jax.ShapeDtypeStruct((SUBLANES, LANES), F32)],
        scratch_shapes=[pltpu.VMEM((SUBLANES, LANES), F32)], compiler_params=_cparams(1))(logits)


def _dispatch_kernel(e_ref, r_ref, ps_ref, h_ref, xz_hbm, xbuf_hbm, sem):
    del xz_hbm
    tr = h_ref.shape[0]
    base = pl.program_id(0) * tr

    def copy(r, k):
        a = (base + r) * 2 + k
        dst = ps_ref[e_ref[a]] + r_ref[a]
        return pltpu.make_async_copy(h_ref.at[pl.ds(r, 1)], xbuf_hbm.at[pl.ds(dst, 1)], sem)

    def start(r, carry):
        copy(r, 0).start()
        copy(r, 1).start()
        return carry
    lax.fori_loop(0, tr, start, 0)

    def wait(r, carry):
        copy(r, 0).wait()
        copy(r, 1).wait()
        return carry
    lax.fori_loop(0, tr, wait, 0)


def _dispatch(h, e_flat, r_flat, pstarts, p_rows):
    t, w = h.shape
    tr = _tile(t, 256, SUBLANES)
    grid_spec = pltpu.PrefetchScalarGridSpec(
        num_scalar_prefetch=3, grid=(t // tr,),
        in_specs=[pl.BlockSpec((tr, w), lambda i, *_: (i, 0)), pl.BlockSpec(memory_space=pl.ANY)],
        out_specs=pl.BlockSpec(memory_space=pl.ANY),
        scratch_shapes=[pltpu.SemaphoreType.DMA(())])
    return pl.pallas_call(
        _dispatch_kernel, grid_spec=grid_spec,
        out_shape=jax.ShapeDtypeStruct((p_rows, w), h.dtype),
        input_output_aliases={4: 0}, compiler_params=_cparams(1),
    )(e_flat, r_flat, pstarts, h, jnp.zeros((p_rows, w), h.dtype))


def _combine_kernel(e_ref, r_ref, ps_ref, x_ref, w_ref, mod_ref, gain_ref, y_hbm, o_ref, ybuf, sem, *, res_row):
    tr = x_ref.shape[0]
    base = pl.program_id(0) * tr

    def copy(r, k):
        a = (base + r) * 2 + k
        src = ps_ref[e_ref[a]] + r_ref[a]
        return pltpu.make_async_copy(y_hbm.at[pl.ds(src, 1)], ybuf.at[k, pl.ds(r, 1)], sem)

    def start(r, carry):
        copy(r, 0).start()
        copy(r, 1).start()
        return carry
    lax.fori_loop(0, tr, start, 0)

    def wait(r, carry):
        copy(r, 0).wait()
        copy(r, 1).wait()
        return carry
    lax.fori_loop(0, tr, wait, 0)

    w = w_ref[...]
    y = ybuf[0] * w[:, 0:1] + ybuf[1] * w[:, 1:2]
    x = x_ref[...] + mod_ref[res_row:res_row + 1, :] * y
    o_ref[...] = _rms(x, gain_ref[...])


def _combine_final(x, ybuf, wts, mod, gain, e_flat, r_flat, pstarts, seq, res_row):
    t, d = x.shape
    tr = _tile(seq, 256, SUBLANES)
    per_b = seq // tr
    grid_spec = pltpu.PrefetchScalarGridSpec(
        num_scalar_prefetch=3, grid=(t // tr,),
        in_specs=[pl.BlockSpec((tr, d), lambda i, *_: (i, 0)), pl.BlockSpec((tr, LANES), lambda i, *_: (i, 0)),
                  pl.BlockSpec((None, SUBLANES, d), lambda i, *_: (i // per_b, 0, 0)),
                  pl.BlockSpec((1, d), lambda i, *_: (0, 0)), pl.BlockSpec(memory_space=pl.ANY)],
        out_specs=pl.BlockSpec((tr, d), lambda i, *_: (i, 0)),
        scratch_shapes=[pltpu.VMEM((2, tr, d), F32), pltpu.SemaphoreType.DMA(())])
    return pl.pallas_call(
        functools.partial(_combine_kernel, res_row=res_row), grid_spec=grid_spec,
        out_shape=jax.ShapeDtypeStruct((t, d), F32), compiler_params=_cparams(1),
    )(e_flat, r_flat, pstarts, x, wts, mod, gain.reshape(1, d), ybuf)


def _split_w_in(w_in, d):
    sizes = (DIL_WIDTH, DIL_WIDTH, DIL_WIDTH, MLA_Q_LORA, MLA_KV_LORA, MLA_ROPE,
             MOBA_HEADS * HEAD_DIM, MOBA_HEADS * HEAD_DIM, MOBA_HEADS * HEAD_DIM,
             DSA_HEADS * HEAD_DIM, HEAD_DIM, HEAD_DIM, IDX_HEADS * IDX_DIM, IDX_DIM, IDX_HEADS, N_BRANCH * d)
    names = ("a_q", "a_k", "a_v", "b_cq", "b_ckv", "b_kr", "c_q", "c_k", "c_v",
             "d_q", "d_k", "d_v", "d_iq", "d_ik", "d_iw", "g")
    assert w_in.shape[1] == sum(sizes)
    parts, o = {}, 0
    for nme, sz in zip(names, sizes):
        parts[nme] = w_in[:, o:o + sz]
        o += sz
    return parts


def _cat_cols(parts, names, pad_to=None):
    cols, offs, o = [], {}, 0
    for nme in names:
        wpart = parts[nme]
        width = -(-wpart.shape[1] // LANES) * LANES
        if width != wpart.shape[1]:
            wpart = jnp.pad(wpart, ((0, 0), (0, width - wpart.shape[1])))
        cols.append(wpart.astype(BF16))
        offs[nme] = o // LANES
        o += width
    if pad_to is not None and o % pad_to:
        cols.append(jnp.zeros((cols[0].shape[0], pad_to - o % pad_to), BF16))
    return jnp.concatenate(cols, axis=1), offs


def _mixer(h, x, mod, bsz, seq, tables, w_in, q_norm, q_up, kv_norm, kv_up, w_branch, w_out):
    t, d = h.shape
    rope_h, rope_iq, rope_ik, rope_mla = tables
    parts = _split_w_in(w_in, d)
    w_rot, ro = _cat_cols(parts, ("c_q", "c_k", "d_q", "d_k"))
    w_plain, po = _cat_cols(parts, ("b_cq", "c_v", "b_ckv", "d_v", "b_kr"), pad_to=512)
    assert ro["d_q"] * LANES % (DSA_HEADS * HEAD_DIM) == 0
    assert po["b_cq"] * LANES % MLA_Q_LORA == 0 and po["b_ckv"] * LANES % MLA_KV_LORA == 0

    rot = _matmul(h, w_rot, out_dtype=BF16, tn_pref=1024, epi="rope", rope=rope_h, half=ROT_DIM // 2)
    plain = _matmul(h, w_plain, out_dtype=BF16, tn_pref=1024)
    gw = DIL_OUT
    a_qk = [_matmul(h, jnp.concatenate([parts["a_q"][:, g * gw:(g + 1) * gw], parts["a_k"][:, g * gw:(g + 1) * gw]],
                                       axis=1).astype(BF16),
                    out_dtype=BF16, tn_pref=1024, epi="rope", rope=rope_h, half=ROT_DIM // 2)
            for g in range(len(DIL_GROUPS))]
    a_v = [_matmul(h, parts["a_v"][:, g * gw:(g + 1) * gw].astype(BF16), out_dtype=BF16, tn_pref=1024)
           for g in range(len(DIL_GROUPS))]
    iq = _matmul(h, parts["d_iq"].astype(BF16), out_dtype=BF16, tn_pref=1024, epi="rope", rope=rope_iq, half=IDX_ROT // 2)
    ik = _matmul(h, _cat_cols(parts, ("d_ik",))[0], out_dtype=F32, tn_pref=LANES, epi="rope", rope=rope_ik, half=IDX_ROT // 2)
    iw = _matmul(h, _cat_cols(parts, ("d_iw",))[0], out_dtype=F32, tn_pref=LANES)
    gates = _matmul(h, parts["g"].astype(BF16), out_dtype=BF16, tn_pref=1024, epi="sigmoid")

    o_a = _dilated_mixture(a_qk, a_v, bsz, seq)

    hq = MLA_NOPE + MLA_ROPE
    wq = q_up.reshape(MLA_Q_LORA, MLA_HEADS, hq)
    wq = jnp.pad(wq, ((0, 0), (0, 0), (0, MLA_QK_PAD - hq))).reshape(MLA_Q_LORA, MLA_HEADS * MLA_QK_PAD).astype(BF16)
    wkv = kv_up.reshape(MLA_KV_LORA, MLA_HEADS, MLA_NOPE + MLA_V)
    wkv = jnp.concatenate([wkv[:, :, :MLA_NOPE].reshape(MLA_KV_LORA, -1), wkv[:, :, MLA_NOPE:].reshape(MLA_KV_LORA, -1)],
                          axis=1).astype(BF16)
    q_cat, k_cat, v_b = _mla_project(plain, po["b_cq"], po["b_ckv"], po["b_kr"], q_norm, wq, kv_norm, wkv, rope_mla)
    o_b = _flash(q_cat, k_cat, v_b, bsz, seq, MLA_HEADS, MLA_QK_PAD, MLA_V, 0, 0, 0, hq ** -0.5)

    kmean = _moba_kmean(rot, bsz, seq, ro["c_k"])
    o_c = _flash(rot, rot, plain, bsz, seq, MOBA_HEADS, HEAD_DIM, HEAD_DIM, ro["c_q"], ro["c_k"], po["c_v"],
                 HEAD_DIM ** -0.5, kmean=kmean)

    o_d = _dsa_attention(rot, plain, iq, ik, iw, bsz, seq, ro["d_q"], ro["d_k"], po["d_v"])

    sizes = (DIL_OUT, MLA_HEADS * MLA_V, MOBA_HEADS * HEAD_DIM, DSA_HEADS * HEAD_DIM)
    projs, o = [], 0
    for sz in sizes:
        projs.append(w_branch[o:o + sz].astype(BF16))
        o += sz
    y = _branch_merge((o_a, o_b, o_c, o_d), projs, gates, d)
    return _matmul(y, w_out.astype(BF16), out_dtype=F32, tn_pref=512, epi="resid", xres=x, mod=mod, res_row=2, seq=seq)


def kernel(x, c, positions, w_ada, b_ada, ada_table_0, mix_norm_0, w_in_0, mla_q_norm_0, mla_q_up_0, mla_kv_norm_0, mla_kv_up_0, w_branch_0, w_out_0, ffn_norm_0, ffn_gate_0, ffn_up_0, ffn_down_0, ada_table_1, mix_norm_1, w_in_1, mla_q_norm_1, mla_q_up_1, mla_kv_norm_1, mla_kv_up_1, w_branch_1, w_out_1, ffn_norm_1, router_1, expert_gate_1, expert_up_1, expert_down_1, final_norm):
    bsz, seq, d = x.shape
    t = bsz * seq
    xf = x.reshape(t, d)
    mod0, mod1 = _ada_mod(c, w_ada, b_ada, ada_table_0, ada_table_1)

    pos_b = jnp.broadcast_to(positions.reshape(t, 1).astype(F32), (t, LANES))
    tables = (_rope_tables(pos_b, _rope_pattern(HEAD_DIM, ROT_DIM, LANES)),
              _rope_tables(pos_b, _rope_pattern(IDX_DIM, IDX_ROT, LANES)),
              _rope_tables(pos_b, _rope_pattern(IDX_DIM, IDX_ROT, IDX_DIM)),
              _rope_tables(pos_b, _rope_pattern(MLA_ROPE, MLA_ROPE, MLA_ROPE)))

    (h,) = _norm_mod(xf, mod0, mix_norm_0, seq, shift_row=0, scale_row=1)
    xf = _mixer(h, xf, mod0, bsz, seq, tables, w_in_0, mla_q_norm_0, mla_q_up_0, mla_kv_norm_0, mla_kv_up_0, w_branch_0, w_out_0)
    (h,) = _norm_mod(xf, mod0, ffn_norm_0, seq, shift_row=3, scale_row=4)
    y = _ffn_dense(h, ffn_gate_0.astype(BF16), ffn_up_0.astype(BF16), ffn_down_0.astype(BF16))

    xf, h = _norm_mod(xf, mod1, mix_norm_1, seq, shift_row=0, scale_row=1, y=y, res_mod=mod0, res_row=5)
    xf = _mixer(h, xf, mod1, bsz, seq, tables, w_in_1, mla_q_norm_1, mla_q_up_1, mla_kv_norm_1, mla_kv_up_1, w_branch_1, w_out_1)
    w_r = jnp.pad(router_1, ((0, 0), (0, LANES - N_EXPERTS))).astype(BF16)
    h, logits = _norm_mod(xf, mod1, ffn_norm_1, seq, shift_row=3, scale_row=4, w_router=w_r, h_dtype=U32)
    info, wts, cnt = _route(logits)
    e_flat = info[:, 0:2].reshape(2 * t)
    r_flat = info[:, 2:4].reshape(2 * t)
    counts = cnt[0, :N_EXPERTS].astype(I32)
    rows = min(MOE_ROWS, t)
    padded = (counts + rows - 1) // rows * rows
    pends = jnp.cumsum(padded).astype(I32)
    pstarts = pends - padded
    p_rows = (-(-2 * t // rows) + N_EXPERTS) * rows
    xbuf = _dispatch(h, e_flat, r_flat, pstarts, p_rows)
    ybuf = _ffn_experts(xbuf, pends, expert_gate_1.astype(BF16), expert_up_1.astype(BF16), expert_down_1.astype(BF16), rows)
    out = _combine_final(xf, ybuf, wts, mod1, final_norm, e_flat, r_flat, pstarts, seq, res_row=5)
    return out.reshape(bsz, seq, d)
```

```python
import functools

import numpy as np
import jax
import jax.numpy as jnp
from jax import lax
from jax.experimental import pallas as pl
from jax.experimental.pallas import tpu as pltpu

F32, BF16, I32, U32 = jnp.float32, jnp.bfloat16, jnp.int32, jnp.uint32

LANES = 128
SUBLANES = 8
VMEM_LIMIT_BYTES = 56 * 1024 * 1024

HEAD_DIM = 128
ROT_DIM = HEAD_DIM // 4
ROPE_THETA = 500000.0
NORM_EPS = 1e-6
MASK_VALUE = -1e30
DIL_GROUPS = ((128, 1), (512, 4), (2048, 16))
DIL_HEADS_PER_GROUP = 4
DIL_HEADS = len(DIL_GROUPS) * DIL_HEADS_PER_GROUP
DIL_WIDTH = DIL_HEADS * HEAD_DIM
DIL_OUT = DIL_HEADS_PER_GROUP * HEAD_DIM
MLA_HEADS = 8
MLA_Q_LORA = 1536
MLA_KV_LORA = 512
MLA_NOPE = 128
MLA_ROPE = 64
MLA_V = 128
MLA_QK_PAD = 256
MOBA_HEADS = 8
MOBA_BLOCK = 256
MOBA_TOPK = 3
DSA_HEADS = 8
DSA_TOPK = 256
IDX_HEADS = 32
IDX_DIM = 64
IDX_ROT = IDX_DIM // 4
N_BRANCH = 4
N_EXPERTS = 8
N_MOD = 6
INT_MIN = -2147483648

FLASH_ROWS = 1024
DSA_TQ = 256
DSA_TK = 512
MOE_ROWS = 512


def _cparams(n_axes):
    return pltpu.CompilerParams(dimension_semantics=("arbitrary",) * n_axes,
                                vmem_limit_bytes=VMEM_LIMIT_BYTES)


def _tile(n, pref, mult=LANES):
    if n <= pref:
        return n
    t = (pref // mult) * mult
    while t > mult and n % t:
        t -= mult
    assert n % t == 0, (n, pref, mult)
    return t


def _nt_dot(a, b):
    return lax.dot_general(a, b, (((1,), (1,)), ((), ())), preferred_element_type=F32)


def _pack_bf16_pairs(x):
    n = x.shape[1]
    bits = lax.bitcast_convert_type(x.astype(BF16).astype(F32), U32)
    return (bits[:, n // 2:] & jnp.uint32(0xFFFF0000)) | (bits[:, :n // 2] >> 16)


def _unpack_bf16_pairs(w):
    lo = lax.bitcast_convert_type(w << 16, F32).astype(BF16)
    hi = lax.bitcast_convert_type(w & jnp.uint32(0xFFFF0000), F32).astype(BF16)
    return lo, hi


def _rope_pattern(group, rot, active):
    half = rot // 2
    freqs = ROPE_THETA ** (-jnp.arange(half, dtype=F32) * (2.0 / rot))
    lane = np.arange(LANES)
    p = lane % group
    on = (lane < active) & (p < rot)
    idx = np.where(on, p % half, 0)
    freq = jnp.where(jnp.asarray(on), freqs[idx], 0.0)
    first = (on & (p < half)).astype(np.float32)
    second = (on & (p >= half)).astype(np.float32)
    pat = jnp.zeros((SUBLANES, LANES), F32)
    return pat.at[0].set(freq).at[1].set(jnp.asarray(first)).at[2].set(jnp.asarray(second))


def _rope_table_kernel(pos_ref, pat_ref, c_ref, sa_ref, sb_ref):
    ang = pos_ref[...] * pat_ref[0:1, :]
    c_ref[...] = jnp.cos(ang)
    sin = jnp.sin(ang)
    sa_ref[...] = -sin * pat_ref[1:2, :]
    sb_ref[...] = sin * pat_ref[2:3, :]


def _rope_tables(pos_b, pat):
    t = pos_b.shape[0]
    ts = _tile(t, 1024)
    spec = pl.BlockSpec((ts, LANES), lambda i: (i, 0))
    sds = jax.ShapeDtypeStruct((t, LANES), F32)
    return pl.pallas_call(
        _rope_table_kernel, grid=(t // ts,),
        in_specs=[spec, pl.BlockSpec((SUBLANES, LANES), lambda i: (0, 0))],
        out_specs=[spec, spec, spec], out_shape=[sds, sds, sds],
        compiler_params=_cparams(1))(pos_b, pat)


def _apply_rope(x, c, sa, sb, half):
    return x * c + pltpu.roll(x, LANES - half, 1) * sa + pltpu.roll(x, half, 1) * sb


def _ada_kernel(c_ref, w_ref, b_ref, t0_ref, t1_ref, o0_ref, o1_ref):
    c = c_ref[...]
    a = (c * jax.nn.sigmoid(c)).astype(BF16)
    acc = jnp.dot(a, w_ref[...].astype(BF16), preferred_element_type=F32) + b_ref[...]
    o0_ref[...] = acc + t0_ref[...]
    o1_ref[...] = acc + t1_ref[...]


def _ada_mod(c, w_ada, b_ada, table0, table1):
    b, d = c.shape
    n = w_ada.shape[1]
    bp = -(-b // SUBLANES) * SUBLANES
    cp = jnp.zeros((bp, d), F32).at[:b].set(c)
    tn = _tile(n, 512)
    row = pl.BlockSpec((1, tn), lambda j: (0, j))
    out = pl.BlockSpec((bp, tn), lambda j: (0, j))
    sds = jax.ShapeDtypeStruct((bp, n), F32)
    m0, m1 = pl.pallas_call(
        _ada_kernel, grid=(n // tn,),
        in_specs=[pl.BlockSpec((bp, d), lambda j: (0, 0)), pl.BlockSpec((d, tn), lambda j: (0, j)), row, row, row],
        out_specs=[out, out], out_shape=[sds, sds], compiler_params=_cparams(1),
    )(cp, w_ada, b_ada.reshape(1, n), table0.reshape(1, n), table1.reshape(1, n))

    def pack(m):
        m = m[:b].reshape(b, N_MOD, d)
        return jnp.concatenate([m, jnp.zeros((b, SUBLANES - N_MOD, d), F32)], axis=1)
    return pack(m0), pack(m1)


def _norm_kernel(*refs, res_row, shift_row, scale_row, with_logits, h_dtype):
    it = iter(refs)
    x_ref = next(it)
    y_ref, rmod_ref = (next(it), next(it)) if res_row is not None else (None, None)
    mod_ref, gain_ref = next(it), next(it)
    wr_ref = next(it) if with_logits else None
    xo_ref = next(it) if res_row is not None else None
    h_ref = next(it)
    lg_ref = next(it) if with_logits else None

    x = x_ref[...]
    if res_row is not None:
        x = x + rmod_ref[res_row:res_row + 1, :] * y_ref[...].astype(F32)
        xo_ref[...] = x
    y = x * lax.rsqrt(jnp.mean(x * x, axis=-1, keepdims=True) + NORM_EPS) * gain_ref[...]
    h = y * (1.0 + mod_ref[scale_row:scale_row + 1, :]) + mod_ref[shift_row:shift_row + 1, :]
    if h_dtype == U32:
        h_ref[...] = _pack_bf16_pairs(h)
    else:
        h_ref[...] = h.astype(h_dtype)
    if with_logits:
        lg_ref[...] = jnp.dot(h.astype(BF16), wr_ref[...], preferred_element_type=F32)


def _norm_mod(x, mod, gain, seq, *, shift_row, scale_row, y=None, res_mod=None, res_row=None, w_router=None,
              h_dtype=BF16):
    t, d = x.shape
    ts = _tile(seq, 256, SUBLANES)
    per_b = seq // ts
    blk = pl.BlockSpec((ts, d), lambda i: (i, 0))
    mod_spec = pl.BlockSpec((None, SUBLANES, d), lambda i: (i // per_b, 0, 0))
    in_specs, args = [blk], [x]
    if res_row is not None:
        in_specs += [blk, mod_spec]
        args += [y, res_mod]
    in_specs += [mod_spec, pl.BlockSpec((1, d), lambda i: (0, 0))]
    args += [mod, gain.reshape(1, d)]
    if w_router is not None:
        in_specs.append(pl.BlockSpec((d, LANES), lambda i: (0, 0)))
        args.append(w_router)
    out_specs, out_shape = [], []
    if res_row is not None:
        out_specs.append(blk)
        out_shape.append(jax.ShapeDtypeStruct((t, d), F32))
    hd = d // 2 if h_dtype == U32 else d
    out_specs.append(pl.BlockSpec((ts, hd), lambda i: (i, 0)))
    out_shape.append(jax.ShapeDtypeStruct((t, hd), h_dtype))
    if w_router is not None:
        out_specs.append(pl.BlockSpec((ts, LANES), lambda i: (i, 0)))
        out_shape.append(jax.ShapeDtypeStruct((t, LANES), F32))
    kern = functools.partial(_norm_kernel, res_row=res_row, shift_row=shift_row, scale_row=scale_row,
                             with_logits=w_router is not None, h_dtype=h_dtype)
    return pl.pallas_call(kern, grid=(t // ts,), in_specs=in_specs, out_specs=out_specs, out_shape=out_shape,
                          compiler_params=_cparams(1))(*args)


def _mm_kernel(*refs, epi, half, res_row):
    a_ref, w_ref = refs[0], refs[1]
    o_ref = refs[-1]
    acc = jnp.dot(a_ref[...], w_ref[...], preferred_element_type=F32)
    if epi == "plain":
        o_ref[...] = acc.astype(o_ref.dtype)
    elif epi == "sigmoid":
        o_ref[...] = jax.nn.sigmoid(acc).astype(o_ref.dtype)
    elif epi == "rope":
        c, sa, sb = refs[2][...], refs[3][...], refs[4][...]
        for ch in range(acc.shape[1] // LANES):
            sl = slice(ch * LANES, (ch + 1) * LANES)
            o_ref[:, sl] = _apply_rope(acc[:, sl], c, sa, sb, half).astype(o_ref.dtype)
    elif epi == "resid":
        xres_ref, mod_ref = refs[2], refs[3]
        o_ref[...] = xres_ref[...] + mod_ref[res_row:res_row + 1, :] * acc
    else:
        raise ValueError(epi)


def _matmul(a, w, *, out_dtype, tn_pref, epi="plain", rope=None, half=0, xres=None, mod=None, res_row=None, seq=None):
    m, k = a.shape
    n = w.shape[1]
    tm = _tile(seq if seq is not None else m, 1024, SUBLANES)
    tn = _tile(n, tn_pref)
    in_specs = [pl.BlockSpec((tm, k), lambda i, j: (i, 0)), pl.BlockSpec((k, tn), lambda i, j: (0, j))]
    args = [a, w]
    if epi == "rope":
        in_specs += [pl.BlockSpec((tm, LANES), lambda i, j: (i, 0))] * 3
        args += list(rope)
    if epi == "resid":
        per_b = seq // tm
        in_specs += [pl.BlockSpec((tm, tn), lambda i, j: (i, j)),
                     pl.BlockSpec((None, SUBLANES, tn), lambda i, j: (i // per_b, 0, j))]
        args += [xres, mod]
    kern = functools.partial(_mm_kernel, epi=epi, half=half, res_row=res_row)
    return pl.pallas_call(
        kern, grid=(m // tm, n // tn), in_specs=in_specs,
        out_specs=pl.BlockSpec((tm, tn), lambda i, j: (i, j)),
        out_shape=jax.ShapeDtypeStruct((m, n), out_dtype), compiler_params=_cparams(2))(*args)


def _dil_kernel(q_ref, kp_ref, ko_ref, vp_ref, vo_ref, o_ref, lse_ref, *, scale):
    i = pl.program_id(2)
    w = q_ref.shape[0]
    row = lax.broadcasted_iota(I32, (w, w), 0)
    col = lax.broadcasted_iota(I32, (w, w), 1)
    for h in range(DIL_HEADS_PER_GROUP):
        sl = slice(h * HEAD_DIM, (h + 1) * HEAD_DIM)
        q = q_ref[:, sl]
        s_o = jnp.where(col <= row, _nt_dot(q, ko_ref[:, sl]) * scale, MASK_VALUE)
        s_p = jnp.where(col >= row, _nt_dot(q, kp_ref[:, sl]) * scale, MASK_VALUE)
        s_p = jnp.where(i > 0, s_p, MASK_VALUE)
        m = jnp.maximum(jnp.max(s_o, axis=1, keepdims=True), jnp.max(s_p, axis=1, keepdims=True))
        p_o = jnp.exp(s_o - m)
        p_p = jnp.exp(s_p - m)
        l = jnp.sum(p_o, axis=1, keepdims=True) + jnp.sum(p_p, axis=1, keepdims=True)
        o = (jnp.dot(p_o.astype(BF16), vo_ref[:, sl], preferred_element_type=F32)
             + jnp.dot(p_p.astype(BF16), vp_ref[:, sl], preferred_element_type=F32))
        o_ref[:, sl] = o / l
        lse_ref[:, sl] = jnp.broadcast_to(m + jnp.log(l), o.shape)


def _dilated_group(qk, v, bsz, seq, g):
    window, dil = DIL_GROUPS[g]
    wsub = window // dil
    n = seq // dil
    assert wsub == HEAD_DIM and n % wsub == 0
    nb = n // wsub
    qk_v = qk.reshape(bsz, n, dil * 2 * DIL_OUT)
    v_v = v.reshape(bsz, n, dil * DIL_OUT)
    blk = (None, wsub, DIL_OUT)
    q_spec = pl.BlockSpec(blk, lambda b, r, i: (b, i, 2 * r))
    k_own = pl.BlockSpec(blk, lambda b, r, i: (b, i, 2 * r + 1))
    k_prev = pl.BlockSpec(blk, lambda b, r, i: (b, jnp.maximum(i - 1, 0), 2 * r + 1))
    v_own = pl.BlockSpec(blk, lambda b, r, i: (b, i, r))
    v_prev = pl.BlockSpec(blk, lambda b, r, i: (b, jnp.maximum(i - 1, 0), r))
    sds = jax.ShapeDtypeStruct((bsz, n, dil * DIL_OUT), F32)
    o, lse = pl.pallas_call(
        functools.partial(_dil_kernel, scale=HEAD_DIM ** -0.5), grid=(bsz, dil, nb),
        in_specs=[q_spec, k_prev, k_own, v_prev, v_own],
        out_specs=[v_own, v_own], out_shape=[sds, sds], compiler_params=_cparams(3),
    )(qk_v, qk_v, qk_v, v_v, v_v)
    return o.reshape(bsz * seq, DIL_OUT), lse.reshape(bsz * seq, DIL_OUT)


def _dil_combine_kernel(o0, o1, o2, l0, l1, l2, out_ref):
    a, b, c = l0[...], l1[...], l2[...]
    m = jnp.maximum(jnp.maximum(a, b), c)
    ea, eb, ec = jnp.exp(a - m), jnp.exp(b - m), jnp.exp(c - m)
    out_ref[...] = ((ea * o0[...] + eb * o1[...] + ec * o2[...]) / (ea + eb + ec)).astype(out_ref.dtype)


def _dilated_mixture(qks, vs, bsz, seq):
    outs, lses = zip(*[_dilated_group(qks[g], vs[g], bsz, seq, g) for g in range(len(DIL_GROUPS))])
    t = bsz * seq
    ts = _tile(t, 1024, SUBLANES)
    spec = pl.BlockSpec((ts, DIL_OUT), lambda i: (i, 0))
    return pl.pallas_call(
        _dil_combine_kernel, grid=(t // ts,), in_specs=[spec] * 6, out_specs=spec,
        out_shape=jax.ShapeDtypeStruct((t, DIL_OUT), BF16), compiler_params=_cparams(1))(*outs, *lses)


def _rms(x, gain):
    return x * lax.rsqrt(jnp.mean(x * x, axis=-1, keepdims=True) + NORM_EPS) * gain


def _mla_q_kernel(cq_ref, gain_ref, w_ref, c_ref, sa_ref, sb_ref, o_ref):
    cqn = _rms(cq_ref[...].astype(F32), gain_ref[...]).astype(BF16)
    acc = jnp.dot(cqn, w_ref[...], preferred_element_type=F32)
    c, sa, sb = c_ref[...], sa_ref[...], sb_ref[...]
    for ch in range(acc.shape[1] // LANES):
        sl = slice(ch * LANES, (ch + 1) * LANES)
        x = acc[:, sl]
        if ch % 2 == 1:
            x = _apply_rope(x, c, sa, sb, MLA_ROPE // 2)
        o_ref[:, sl] = x.astype(o_ref.dtype)


def _mla_kv_kernel(ckv_ref, gain_ref, w_ref, kr_ref, c_ref, sa_ref, sb_ref, k_ref, v_ref):
    ckvn = _rms(ckv_ref[...].astype(F32), gain_ref[...]).astype(BF16)
    acc = jnp.dot(ckvn, w_ref[...], preferred_element_type=F32)
    kpe = _apply_rope(kr_ref[...].astype(F32), c_ref[...], sa_ref[...], sb_ref[...], MLA_ROPE // 2).astype(k_ref.dtype)
    hw = MLA_HEADS * MLA_NOPE
    for h in range(MLA_HEADS):
        k_ref[:, h * MLA_QK_PAD:h * MLA_QK_PAD + MLA_NOPE] = acc[:, h * MLA_NOPE:(h + 1) * MLA_NOPE].astype(k_ref.dtype)
        k_ref[:, h * MLA_QK_PAD + MLA_NOPE:(h + 1) * MLA_QK_PAD] = kpe
    v_ref[...] = acc[:, hw:].astype(v_ref.dtype)


def _mla_project(plain, cq_col, ckv_col, kr_col, q_gain, wq, kv_gain, wkv, rope):
    t = plain.shape[0]
    ts = _tile(t, 512, SUBLANES)
    tab = pl.BlockSpec((ts, LANES), lambda i: (i, 0))
    nq = MLA_HEADS * MLA_QK_PAD
    q_cat = pl.pallas_call(
        _mla_q_kernel, grid=(t // ts,),
        in_specs=[pl.BlockSpec((ts, MLA_Q_LORA), lambda i: (i, cq_col * LANES // MLA_Q_LORA)),
                  pl.BlockSpec((1, MLA_Q_LORA), lambda i: (0, 0)),
                  pl.BlockSpec((MLA_Q_LORA, nq), lambda i: (0, 0)), tab, tab, tab],
        out_specs=pl.BlockSpec((ts, nq), lambda i: (i, 0)),
        out_shape=jax.ShapeDtypeStruct((t, nq), BF16), compiler_params=_cparams(1),
    )(plain, q_gain.reshape(1, -1), wq, *rope)
    nv = MLA_HEADS * MLA_V
    k_cat, v = pl.pallas_call(
        _mla_kv_kernel, grid=(t // ts,),
        in_specs=[pl.BlockSpec((ts, MLA_KV_LORA), lambda i: (i, ckv_col * LANES // MLA_KV_LORA)),
                  pl.BlockSpec((1, MLA_KV_LORA), lambda i: (0, 0)),
                  pl.BlockSpec((MLA_KV_LORA, MLA_HEADS * MLA_NOPE + nv), lambda i: (0, 0)),
                  pl.BlockSpec((ts, LANES), lambda i: (i, kr_col)), tab, tab, tab],
        out_specs=[pl.BlockSpec((ts, nq), lambda i: (i, 0)), pl.BlockSpec((ts, nv), lambda i: (i, 0))],
        out_shape=[jax.ShapeDtypeStruct((t, nq), BF16), jax.ShapeDtypeStruct((t, nv), BF16)],
        compiler_params=_cparams(1),
    )(plain, kv_gain.reshape(1, -1), wkv, plain, *rope)
    return q_cat, k_cat, v


def _softmax_step(s, v, m_sc, l_sc, acc_sc, rows):
    tiles = [s[:, j * LANES:(j + 1) * LANES] for j in range(s.shape[1] // LANES)]
    m_prev = m_sc[rows, :]
    m_new = jnp.maximum(m_prev, jnp.max(functools.reduce(jnp.maximum, tiles), axis=1, keepdims=True))
    alpha = jnp.exp(m_prev - m_new)
    ps = [jnp.exp(x - m_new) for x in tiles]
    l_sc[rows, :] = alpha * l_sc[rows, :] + jnp.sum(functools.reduce(jnp.add, ps), axis=1, keepdims=True)
    p = jnp.concatenate([x.astype(BF16) for x in ps], axis=1)
    acc_sc[rows, :] = alpha * acc_sc[rows, :] + jnp.dot(p, v, preferred_element_type=F32)
    m_sc[rows, :] = m_new


def _flash_kernel(*refs, scale, moba):
    if moba:
        q_ref, k_ref, v_ref, km_ref, o_ref, m_sc, l_sc, acc_sc, sel_sc = refs
    else:
        q_ref, k_ref, v_ref, o_ref, m_sc, l_sc, acc_sc = refs
    qi, step = pl.program_id(2), pl.program_id(3)
    kv = qi - step
    tq, tk = q_ref.shape[0], k_ref.shape[0]

    @pl.when(step == 0)
    def _init():
        m_sc[...] = jnp.full(m_sc.shape, MASK_VALUE, F32)
        l_sc[...] = jnp.zeros(l_sc.shape, F32)
        acc_sc[...] = jnp.zeros(acc_sc.shape, F32)
        if moba:
            lane = lax.broadcasted_iota(I32, (tq, LANES), 1)
            own = (qi * tq + lax.broadcasted_iota(I32, (tq, LANES), 0)) // MOBA_BLOCK
            valid = lane < own
            g = jnp.where(valid, _nt_dot(q_ref[...], km_ref[...]), MASK_VALUE)
            sel = jnp.zeros((tq, LANES), F32)
            lane_f = lane.astype(F32)
            for _ in range(MOBA_TOPK):
                mx = jnp.max(g, axis=1, keepdims=True)
                first = jnp.min(jnp.where(g == mx, lane_f, float(LANES)), axis=1, keepdims=True)
                pick = lane_f == first
                sel = jnp.where(pick, 1.0, sel)
                g = jnp.where(pick, -jnp.inf, g)
            sel_sc[...] = jnp.where(valid, sel, 0.0)

    rc = min(FLASH_ROWS, tq)

    def block_flags(rows):
        lane = lax.broadcasted_iota(I32, (rc, LANES), 1)
        sel = sel_sc[rows, :]
        nsub = tk // MOBA_BLOCK
        flags = []
        for c in range(nsub):
            f = jnp.sum(jnp.where(lane == kv * nsub + c, sel, 0.0), axis=1, keepdims=True)
            flags.append(jnp.broadcast_to(f, (rc, MOBA_BLOCK)))
        return flags[0] if nsub == 1 else jnp.concatenate(flags, axis=1)

    @pl.when(step == 0)
    def _diagonal():
        for r0 in range(0, tq, rc):
            rows = slice(r0, r0 + rc)
            s = _nt_dot(q_ref[rows, :], k_ref[...]) * scale
            row = r0 + lax.broadcasted_iota(I32, (rc, tk), 0)
            col = lax.broadcasted_iota(I32, (rc, tk), 1)
            ok = col <= row
            if moba:
                same = (col // MOBA_BLOCK) == (row // MOBA_BLOCK)
                ok = jnp.logical_and(ok, jnp.logical_or(same, block_flags(rows) > 0.0))
            _softmax_step(jnp.where(ok, s, MASK_VALUE), v_ref[...], m_sc, l_sc, acc_sc, rows)

    @pl.when(jnp.logical_and(step > 0, step <= qi))
    def _past():
        for r0 in range(0, tq, rc):
            rows = slice(r0, r0 + rc)
            s = _nt_dot(q_ref[rows, :], k_ref[...]) * scale
            if moba:
                s = jnp.where(block_flags(rows) > 0.0, s, MASK_VALUE)
            _softmax_step(s, v_ref[...], m_sc, l_sc, acc_sc, rows)

    @pl.when(step == pl.num_programs(3) - 1)
    def _fin():
        o_ref[...] = (acc_sc[...] / l_sc[:, 0:1]).astype(o_ref.dtype)


def _flash(q, k, v, bsz, seq, heads, dq, dv, q_col, k_col, v_col, scale, kmean=None):
    t = bsz * seq
    tq = _tile(seq, 1024, MOBA_BLOCK)
    nq = seq // tq
    in_specs = [pl.BlockSpec((tq, dq), lambda b, h, i, s: (b * nq + i, q_col + h)),
                pl.BlockSpec((tq, dq), lambda b, h, i, s: (b * nq + jnp.maximum(i - s, 0), k_col + h)),
                pl.BlockSpec((tq, dv), lambda b, h, i, s: (b * nq + jnp.maximum(i - s, 0), v_col + h))]
    args = [q, k, v]
    scratch = [pltpu.VMEM((tq, LANES), F32), pltpu.VMEM((tq, LANES), F32), pltpu.VMEM((tq, dv), F32)]
    if kmean is not None:
        in_specs.append(pl.BlockSpec((None, None, LANES, HEAD_DIM), lambda b, h, i, s: (b, h, 0, 0)))
        args.append(kmean)
        scratch.append(pltpu.VMEM((tq, LANES), F32))
    return pl.pallas_call(
        functools.partial(_flash_kernel, scale=scale, moba=kmean is not None),
        grid=(bsz, heads, nq, nq), in_specs=in_specs,
        out_specs=pl.BlockSpec((tq, dv), lambda b, h, i, s: (b * nq + i, h)),
        out_shape=jax.ShapeDtypeStruct((t, heads * dv), BF16),
        scratch_shapes=scratch, compiler_params=_cparams(4))(*args)


def _kmean_kernel(k_ref, o_ref):
    s = k_ref.shape[0]
    nb = s // MOBA_BLOCK
    k = k_ref[...].astype(F32).reshape(nb, MOBA_BLOCK, HEAD_DIM)
    mean = jnp.sum(k, axis=1) * (1.0 / MOBA_BLOCK)
    full = jnp.concatenate([mean, jnp.zeros((LANES - nb, HEAD_DIM), F32)], axis=0)
    o_ref[...] = full.astype(o_ref.dtype)


def _moba_kmean(rot, bsz, seq, k_col):
    assert seq % MOBA_BLOCK == 0 and seq // MOBA_BLOCK <= LANES
    return pl.pallas_call(
        _kmean_kernel, grid=(bsz, MOBA_HEADS),
        in_specs=[pl.BlockSpec((seq, HEAD_DIM), lambda b, h: (b, k_col + h))],
        out_specs=pl.BlockSpec((None, None, LANES, HEAD_DIM), lambda b, h: (b, h, 0, 0)),
        out_shape=jax.ShapeDtypeStruct((bsz, MOBA_HEADS, LANES, HEAD_DIM), BF16),
        compiler_params=_cparams(2))(rot)


def _dsa_kernel(q_ref, iq_ref, iw_ref, ikbd_ref, k_ref, v_ref, o_ref, key_sc, qs_sc, m_sc, l_sc, acc_sc, *, n_keep, scale,
                wscale):
    c = pl.program_id(1)
    tq, tk = DSA_TQ, DSA_TK
    nkt = (c * tq) // tk + 1
    rowg = c * tq + lax.broadcasted_iota(I32, (tq, tk), 0)
    col = lax.broadcasted_iota(I32, (tq, tk), 1)
    w = iw_ref[...] * wscale

    def score_tile(kt, carry):
        ikb = ikbd_ref[kt]
        sc = jnp.zeros((tq, tk), F32)
        for j in range(IDX_HEADS // 2):
            rel = jnp.maximum(_nt_dot(iq_ref[:, j * LANES:(j + 1) * LANES], ikb), 0.0)
            sc = sc + rel[:, :tk] * w[:, 2 * j:2 * j + 1] + rel[:, tk:] * w[:, 2 * j + 1:2 * j + 2]
        bits = lax.bitcast_convert_type(sc, I32)
        key = bits ^ ((bits >> 31) & 0x7FFFFFFF)
        key_sc[kt] = jnp.where(kt * tk + col <= rowg, key, INT_MIN)
        return carry
    lax.fori_loop(0, nkt, score_tile, 0)

    def count_ge(cand):
        def body(kt, acc):
            ge = jnp.where(key_sc[kt] >= cand, 1.0, 0.0)
            for j in range(tk // LANES):
                acc = acc + ge[:, j * LANES:(j + 1) * LANES]
            return acc
        acc = lax.fori_loop(0, nkt, body, jnp.zeros((tq, LANES), F32))
        return jnp.sum(acc, axis=1, keepdims=True)

    keep = jnp.float32(n_keep)
    thr = jnp.where(count_ge(jnp.zeros((tq, 1), I32)) >= keep, 0, INT_MIN).astype(I32)

    def search(it, thr):
        cand = thr | (jnp.int32(1) << (30 - it))
        return jnp.where(count_ge(cand) >= keep, cand, thr)
    thr = lax.fori_loop(0, 31, search, thr)
    thr = jnp.maximum(thr, INT_MIN + 1)

    for h in range(DSA_HEADS):
        qs_sc[h * tq:(h + 1) * tq, :] = q_ref[:, h * HEAD_DIM:(h + 1) * HEAD_DIM]
    m_sc[...] = jnp.full(m_sc.shape, MASK_VALUE, F32)
    l_sc[...] = jnp.zeros(l_sc.shape, F32)
    acc_sc[...] = jnp.zeros(acc_sc.shape, F32)

    def attend(kt, carry):
        sel = key_sc[kt] >= thr
        ks = k_ref[pl.ds(pl.multiple_of(kt * tk, tk), tk), :]
        vs = v_ref[pl.ds(pl.multiple_of(kt * tk, tk), tk), :]
        s = _nt_dot(qs_sc[...], ks) * scale
        s = jnp.concatenate([jnp.where(sel, s[h * tq:(h + 1) * tq, :], MASK_VALUE) for h in range(DSA_HEADS)], axis=0)
        _softmax_step(s, vs, m_sc, l_sc, acc_sc, slice(None))
        return carry
    lax.fori_loop(0, nkt, attend, 0)

    for h in range(DSA_HEADS):
        rows = slice(h * tq, (h + 1) * tq)
        o_ref[:, h * HEAD_DIM:(h + 1) * HEAD_DIM] = (acc_sc[rows, :] / l_sc[rows, 0:1]).astype(o_ref.dtype)


def _dsa_attention(rot, k, plain, iq, ik, iw, bsz, seq, q_col, v_col):
    t = bsz * seq
    assert seq % DSA_TK == 0
    nkt = seq // DSA_TK
    nch = seq // DSA_TQ
    ika = ik.astype(BF16).reshape(bsz, nkt, DSA_TK, LANES)
    ikbd = jnp.concatenate([ika, jnp.roll(ika, IDX_DIM, axis=-1)], axis=2)
    n_keep = min(DSA_TOPK, seq // 4)
    qw = DSA_HEADS * HEAD_DIM
    iqw = IDX_HEADS * IDX_DIM
    kern = functools.partial(_dsa_kernel, n_keep=n_keep, scale=HEAD_DIM ** -0.5,
                             wscale=(IDX_HEADS ** -0.5) * (IDX_DIM ** -0.5))
    return pl.pallas_call(
        kern, grid=(bsz, nch),
        in_specs=[pl.BlockSpec((DSA_TQ, qw), lambda b, c: (b * nch + c, q_col * HEAD_DIM // qw)),
                  pl.BlockSpec((DSA_TQ, iqw), lambda b, c: (b * nch + c, 0)),
                  pl.BlockSpec((DSA_TQ, LANES), lambda b, c: (b * nch + c, 0)),
                  pl.BlockSpec((None, nkt, 2 * DSA_TK, LANES), lambda b, c: (b, 0, 0, 0)),
                  pl.BlockSpec((seq, HEAD_DIM), lambda b, c: (b, 0)),
                  pl.BlockSpec((seq, HEAD_DIM), lambda b, c: (b, v_col))],
        out_specs=pl.BlockSpec((DSA_TQ, qw), lambda b, c: (b * nch + c, 0)),
        out_shape=jax.ShapeDtypeStruct((t, qw), BF16),
        scratch_shapes=[pltpu.VMEM((nkt, DSA_TQ, DSA_TK), I32), pltpu.VMEM((DSA_HEADS * DSA_TQ, HEAD_DIM), BF16),
                        pltpu.VMEM((DSA_HEADS * DSA_TQ, LANES), F32), pltpu.VMEM((DSA_HEADS * DSA_TQ, LANES), F32),
                        pltpu.VMEM((DSA_HEADS * DSA_TQ, HEAD_DIM), F32)],
        compiler_params=_cparams(2))(rot, iq, iw, ikbd, k, plain)


def _branch_kernel(oa, ob, oc, od, pa, pb, pc, pd, ga, gb, gc, gd, y_ref):
    y = None
    for o, p, g in ((oa, pa, ga), (ob, pb, gb), (oc, pc, gc), (od, pd, gd)):
        term = g[...].astype(F32) * jnp.dot(o[...], p[...], preferred_element_type=F32)
        y = term if y is None else y + term
    y_ref[...] = y.astype(y_ref.dtype)


def _branch_merge(outs, projs, gates, d):
    t = outs[0].shape[0]
    tm = _tile(t, 1024, SUBLANES)
    tn = _tile(d, 512)
    nj = d // tn
    in_specs = [pl.BlockSpec((tm, o.shape[1]), lambda i, j: (i, 0)) for o in outs]
    in_specs += [pl.BlockSpec((p.shape[0], tn), lambda i, j: (0, j)) for p in projs]
    in_specs += [pl.BlockSpec((tm, tn), functools.partial(lambda i, j, k: (i, k * nj + j), k=k)) for k in range(N_BRANCH)]
    return pl.pallas_call(
        _branch_kernel, grid=(t // tm, nj), in_specs=in_specs,
        out_specs=pl.BlockSpec((tm, tn), lambda i, j: (i, j)),
        out_shape=jax.ShapeDtypeStruct((t, d), BF16), compiler_params=_cparams(2),
    )(*outs, *projs, gates, gates, gates, gates)


def _glu_kernel(*refs, n_prefetch):
    pends_ref = refs[0] if n_prefetch else None
    x_ref, wg_ref, wu_ref, o_ref = refs[n_prefetch:n_prefetch + 4]

    def compute():
        if x_ref.dtype == U32:
            xb_sc = refs[n_prefetch + 4]

            @pl.when(pl.program_id(1) == 0)
            def _():
                lo, hi = _unpack_bf16_pairs(x_ref[...])
                xb_sc[:, :lo.shape[1]] = lo
                xb_sc[:, lo.shape[1]:] = hi
            x = xb_sc[...]
        else:
            x = x_ref[...]
        g = jnp.dot(x, wg_ref[...], preferred_element_type=F32)
        u = jnp.dot(x, wu_ref[...], preferred_element_type=F32)
        o_ref[...] = (g * jax.nn.sigmoid(g) * u).astype(o_ref.dtype)

    if n_prefetch:
        used = pl.program_id(0) * x_ref.shape[0] < pends_ref[N_EXPERTS - 1]
        pl.when(used)(compute)

        @pl.when(jnp.logical_not(used))
        def _():
            o_ref[...] = jnp.zeros(o_ref.shape, o_ref.dtype)
    else:
        compute()


def _mmk_kernel(a_ref, w_ref, o_ref, acc_sc):
    k = pl.program_id(2)
    part = jnp.dot(a_ref[...], w_ref[...], preferred_element_type=F32)

    @pl.when(k == 0)
    def _():
        acc_sc[...] = part

    @pl.when(k > 0)
    def _():
        acc_sc[...] += part

    @pl.when(k == pl.num_programs(2) - 1)
    def _():
        o_ref[...] = acc_sc[...].astype(o_ref.dtype)


def _ffn_dense(h, wg, wu, wd):
    t, d = h.shape
    ff = wg.shape[1]
    tm = _tile(t, 1024, SUBLANES)
    tf = _tile(ff, 512)
    a = pl.pallas_call(
        functools.partial(_glu_kernel, n_prefetch=0), grid=(t // tm, ff // tf),
        in_specs=[pl.BlockSpec((tm, d), lambda i, f: (i, 0)), pl.BlockSpec((d, tf), lambda i, f: (0, f)),
                  pl.BlockSpec((d, tf), lambda i, f: (0, f))],
        out_specs=pl.BlockSpec((tm, tf), lambda i, f: (i, f)),
        out_shape=jax.ShapeDtypeStruct((t, ff), BF16), compiler_params=_cparams(2))(h, wg, wu)
    tn = _tile(d, 1024)
    tk = _tile(ff, 2048)
    return pl.pallas_call(
        _mmk_kernel, grid=(t // tm, d // tn, ff // tk),
        in_specs=[pl.BlockSpec((tm, tk), lambda i, j, k: (i, k)), pl.BlockSpec((tk, tn), lambda i, j, k: (k, j))],
        out_specs=pl.BlockSpec((tm, tn), lambda i, j, k: (i, j)),
        out_shape=jax.ShapeDtypeStruct((t, d), BF16),
        scratch_shapes=[pltpu.VMEM((tm, tn), F32)], compiler_params=_cparams(3))(a, wd)


def _down_experts_kernel(pends_ref, a_ref, w_ref, o_ref):
    used = pl.program_id(0) * a_ref.shape[0] < pends_ref[N_EXPERTS - 1]

    @pl.when(used)
    def _():
        o_ref[...] = jnp.dot(a_ref[...], w_ref[...], preferred_element_type=F32)

    @pl.when(jnp.logical_not(used))
    def _():
        o_ref[...] = jnp.zeros(o_ref.shape, o_ref.dtype)


def _ffn_experts(xbuf, pends, wg, wu, wd, rows):
    p = xbuf.shape[0]
    d, ff = wg.shape[1], wg.shape[2]
    tf = _tile(ff, 512)

    def expert(i, pends_ref):
        e = jnp.int32(0)
        for j in range(N_EXPERTS - 1):
            e = e + (i * rows >= pends_ref[j]).astype(I32)
        return e

    a = pl.pallas_call(
        functools.partial(_glu_kernel, n_prefetch=1),
        grid_spec=pltpu.PrefetchScalarGridSpec(
            num_scalar_prefetch=1, grid=(p // rows, ff // tf),
            in_specs=[pl.BlockSpec((rows, d // 2), lambda i, f, pe: (i, 0)),
                      pl.BlockSpec((None, d, tf), lambda i, f, pe: (expert(i, pe), 0, f)),
                      pl.BlockSpec((None, d, tf), lambda i, f, pe: (expert(i, pe), 0, f))],
            out_specs=pl.BlockSpec((rows, tf), lambda i, f, pe: (i, f)),
            scratch_shapes=[pltpu.VMEM((rows, d), BF16)]),
        out_shape=jax.ShapeDtypeStruct((p, ff), BF16), compiler_params=_cparams(2))(pends, xbuf, wg, wu)
    tn = _tile(d, 1024)
    return pl.pallas_call(
        _down_experts_kernel,
        grid_spec=pltpu.PrefetchScalarGridSpec(
            num_scalar_prefetch=1, grid=(p // rows, d // tn),
            in_specs=[pl.BlockSpec((rows, ff), lambda i, j, pe: (i, 0)),
                      pl.BlockSpec((None, ff, tn), lambda i, j, pe: (expert(i, pe), 0, j))],
            out_specs=pl.BlockSpec((rows, tn), lambda i, j, pe: (i, j))),
        out_shape=jax.ShapeDtypeStruct((p, d), F32), compiler_params=_cparams(2))(pends, a, wd)


def _route_kernel(lg_ref, info_ref, w_ref, cnt_ref, carry_sc):
    i = pl.program_id(0)

    @pl.when(i == 0)
    def _():
        carry_sc[...] = jnp.zeros(carry_sc.shape, F32)

    tr = lg_ref.shape[0]
    lane = lax.broadcasted_iota(I32, (tr, LANES), 1)
    lg = jnp.where(lane < N_EXPERTS, lg_ref[...], -jnp.inf)
    m1 = jnp.max(lg, axis=1, keepdims=True)
    i1 = jnp.min(jnp.where(lg == m1, lane, LANES), axis=1, keepdims=True)
    lg2 = jnp.where(lane == i1, -jnp.inf, lg)
    m2 = jnp.max(lg2, axis=1, keepdims=True)
    i2 = jnp.min(jnp.where(lg2 == m2, lane, LANES), axis=1, keepdims=True)
    e = jnp.exp(m2 - m1)
    w1 = 1.0 / (1.0 + e)
    w2 = e * w1
    oh = jnp.where(jnp.logical_or(lane == i1, lane == i2), 1.0, 0.0)
    tri = jnp.where(lax.broadcasted_iota(I32, (tr, tr), 1) < lax.broadcasted_iota(I32, (tr, tr), 0), 1.0, 0.0)
    before = jnp.dot(tri.astype(BF16), oh.astype(BF16), preferred_element_type=F32) + carry_sc[0:1, :]
    r1 = jnp.sum(jnp.where(lane == i1, before, 0.0), axis=1, keepdims=True).astype(I32)
    r2 = jnp.sum(jnp.where(lane == i2, before, 0.0), axis=1, keepdims=True).astype(I32)
    carry_sc[0:1, :] = carry_sc[0:1, :] + jnp.sum(oh, axis=0, keepdims=True)
    info_ref[...] = jnp.where(lane == 0, i1, jnp.where(lane == 1, i2, jnp.where(lane == 2, r1, jnp.where(lane == 3, r2, 0))))
    w_ref[...] = jnp.where(lane == 0, w1, jnp.where(lane == 1, w2, 0.0))
    cnt_ref[...] = carry_sc[...]


def _route(logits):
    t = logits.shape[0]
    tr = _tile(t, 256, SUBLANES)
    blk = pl.BlockSpec((tr, LANES), lambda i: (i, 0))
    return pl.pallas_call(
        _route_kernel, grid=(t // tr,), in_specs=[blk],
        out_specs=[blk, blk, pl.BlockSpec((SUBLANES, LANES), lambda i: (0, 0))],
        out_shape=[jax.ShapeDtypeStruct((t, LANES), I32), jax.ShapeDtypeStruct((t, LANES), F32),
                   jax.ShapeDtypeStruct((SUBLANES, LANES), F32)],
        scratch_shapes=[pltpu.VMEM((SUBLANES, LANES), F32)], compiler_params=_cparams(1))(logits)


def _dispatch_kernel(e_ref, r_ref, ps_ref, h_ref, xz_hbm, xbuf_hbm, sem):
    del xz_hbm
    tr = h_ref.shape[0]
    base = pl.program_id(0) * tr

    def copy(r, k):
        a = (base + r) * 2 + k
        dst = ps_ref[e_ref[a]] + r_ref[a]
        return pltpu.make_async_copy(h_ref.at[pl.ds(r, 1)], xbuf_hbm.at[pl.ds(dst, 1)], sem)

    def start(r, carry):
        copy(r, 0).start()
        copy(r, 1).start()
        return carry
    lax.fori_loop(0, tr, start, 0)

    def wait(r, carry):
        copy(r, 0).wait()
        copy(r, 1).wait()
        return carry
    lax.fori_loop(0, tr, wait, 0)


def _dispatch(h, e_flat, r_flat, pstarts, p_rows):
    t, w = h.shape
    tr = _tile(t, 256, SUBLANES)
    grid_spec = pltpu.PrefetchScalarGridSpec(
        num_scalar_prefetch=3, grid=(t // tr,),
        in_specs=[pl.BlockSpec((tr, w), lambda i, *_: (i, 0)), pl.BlockSpec(memory_space=pl.ANY)],
        out_specs=pl.BlockSpec(memory_space=pl.ANY),
        scratch_shapes=[pltpu.SemaphoreType.DMA(())])
    return pl.pallas_call(
        _dispatch_kernel, grid_spec=grid_spec,
        out_shape=jax.ShapeDtypeStruct((p_rows, w), h.dtype),
        input_output_aliases={4: 0}, compiler_params=_cparams(1),
    )(e_flat, r_flat, pstarts, h, jnp.zeros((p_rows, w), h.dtype))


def _combine_kernel(e_ref, r_ref, ps_ref, x_ref, w_ref, mod_ref, gain_ref, y_hbm, o_ref, ybuf, sem, *, res_row):
    tr = x_ref.shape[0]
    base = pl.program_id(0) * tr

    def copy(r, k):
        a = (base + r) * 2 + k
        src = ps_ref[e_ref[a]] + r_ref[a]
        return pltpu.make_async_copy(y_hbm.at[pl.ds(src, 1)], ybuf.at[k, pl.ds(r, 1)], sem)

    def start(r, carry):
        copy(r, 0).start()
        copy(r, 1).start()
        return carry
    lax.fori_loop(0, tr, start, 0)

    def wait(r, carry):
        copy(r, 0).wait()
        copy(r, 1).wait()
        return carry
    lax.fori_loop(0, tr, wait, 0)

    w = w_ref[...]
    y = ybuf[0] * w[:, 0:1] + ybuf[1] * w[:, 1:2]
    x = x_ref[...] + mod_ref[res_row:res_row + 1, :] * y
    o_ref[...] = _rms(x, gain_ref[...])


def _combine_final(x, ybuf, wts, mod, gain, e_flat, r_flat, pstarts, seq, res_row):
    t, d = x.shape
    tr = _tile(seq, 256, SUBLANES)
    per_b = seq // tr
    grid_spec = pltpu.PrefetchScalarGridSpec(
        num_scalar_prefetch=3, grid=(t // tr,),
        in_specs=[pl.BlockSpec((tr, d), lambda i, *_: (i, 0)), pl.BlockSpec((tr, LANES), lambda i, *_: (i, 0)),
                  pl.BlockSpec((None, SUBLANES, d), lambda i, *_: (i // per_b, 0, 0)),
                  pl.BlockSpec((1, d), lambda i, *_: (0, 0)), pl.BlockSpec(memory_space=pl.ANY)],
        out_specs=pl.BlockSpec((tr, d), lambda i, *_: (i, 0)),
        scratch_shapes=[pltpu.VMEM((2, tr, d), F32), pltpu.SemaphoreType.DMA(())])
    return pl.pallas_call(
        functools.partial(_combine_kernel, res_row=res_row), grid_spec=grid_spec,
        out_shape=jax.ShapeDtypeStruct((t, d), F32), compiler_params=_cparams(1),
    )(e_flat, r_flat, pstarts, x, wts, mod, gain.reshape(1, d), ybuf)


def _split_w_in(w_in, d):
    sizes = (DIL_WIDTH, DIL_WIDTH, DIL_WIDTH, MLA_Q_LORA, MLA_KV_LORA, MLA_ROPE,
             MOBA_HEADS * HEAD_DIM, MOBA_HEADS * HEAD_DIM, MOBA_HEADS * HEAD_DIM,
             DSA_HEADS * HEAD_DIM, HEAD_DIM, HEAD_DIM, IDX_HEADS * IDX_DIM, IDX_DIM, IDX_HEADS, N_BRANCH * d)
    names = ("a_q", "a_k", "a_v", "b_cq", "b_ckv", "b_kr", "c_q", "c_k", "c_v",
             "d_q", "d_k", "d_v", "d_iq", "d_ik", "d_iw", "g")
    assert w_in.shape[1] == sum(sizes)
    parts, o = {}, 0
    for nme, sz in zip(names, sizes):
        parts[nme] = w_in[:, o:o + sz]
        o += sz
    return parts


def _cat_cols(parts, names, pad_to=None):
    cols, offs, o = [], {}, 0
    for nme in names:
        wpart = parts[nme]
        width = -(-wpart.shape[1] // LANES) * LANES
        if width != wpart.shape[1]:
            wpart = jnp.pad(wpart, ((0, 0), (0, width - wpart.shape[1])))
        cols.append(wpart.astype(BF16))
        offs[nme] = o // LANES
        o += width
    if pad_to is not None and o % pad_to:
        cols.append(jnp.zeros((cols[0].shape[0], pad_to - o % pad_to), BF16))
    return jnp.concatenate(cols, axis=1), offs


def _mixer(h, x, mod, bsz, seq, tables, w_in, q_norm, q_up, kv_norm, kv_up, w_branch, w_out):
    t, d = h.shape
    rope_h, rope_iq, rope_ik, rope_mla = tables
    parts = _split_w_in(w_in, d)
    w_rot, ro = _cat_cols(parts, ("c_q", "c_k", "d_q"))
    w_plain, po = _cat_cols(parts, ("b_cq", "c_v", "b_ckv", "d_v", "b_kr"), pad_to=512)
    assert ro["d_q"] * LANES % (DSA_HEADS * HEAD_DIM) == 0
    assert po["b_cq"] * LANES % MLA_Q_LORA == 0 and po["b_ckv"] * LANES % MLA_KV_LORA == 0

    rot = _matmul(h, w_rot, out_dtype=BF16, tn_pref=1024, epi="rope", rope=rope_h, half=ROT_DIM // 2)
    plain = _matmul(h, w_plain, out_dtype=BF16, tn_pref=512)
    d_k = _matmul(h, parts["d_k"].astype(BF16), out_dtype=BF16, tn_pref=LANES, epi="rope", rope=rope_h, half=ROT_DIM // 2)
    gw = DIL_OUT
    a_qk = [_matmul(h, jnp.concatenate([parts["a_q"][:, g * gw:(g + 1) * gw], parts["a_k"][:, g * gw:(g + 1) * gw]],
                                       axis=1).astype(BF16),
                    out_dtype=BF16, tn_pref=1024, epi="rope", rope=rope_h, half=ROT_DIM // 2)
            for g in range(len(DIL_GROUPS))]
    a_v = [_matmul(h, parts["a_v"][:, g * gw:(g + 1) * gw].astype(BF16), out_dtype=BF16, tn_pref=1024)
           for g in range(len(DIL_GROUPS))]
    iq = _matmul(h, parts["d_iq"].astype(BF16), out_dtype=BF16, tn_pref=1024, epi="rope", rope=rope_iq, half=IDX_ROT // 2)
    ik = _matmul(h, _cat_cols(parts, ("d_ik",))[0], out_dtype=F32, tn_pref=LANES, epi="rope", rope=rope_ik, half=IDX_ROT // 2)
    iw = _matmul(h, _cat_cols(parts, ("d_iw",))[0], out_dtype=F32, tn_pref=LANES)
    gates = _matmul(h, parts["g"].astype(BF16), out_dtype=BF16, tn_pref=1024, epi="sigmoid")

    o_a = _dilated_mixture(a_qk, a_v, bsz, seq)

    hq = MLA_NOPE + MLA_ROPE
    wq = q_up.reshape(MLA_Q_LORA, MLA_HEADS, hq)
    wq = jnp.pad(wq, ((0, 0), (0, 0), (0, MLA_QK_PAD - hq))).reshape(MLA_Q_LORA, MLA_HEADS * MLA_QK_PAD).astype(BF16)
    wkv = kv_up.reshape(MLA_KV_LORA, MLA_HEADS, MLA_NOPE + MLA_V)
    wkv = jnp.concatenate([wkv[:, :, :MLA_NOPE].reshape(MLA_KV_LORA, -1), wkv[:, :, MLA_NOPE:].reshape(MLA_KV_LORA, -1)],
                          axis=1).astype(BF16)
    q_cat, k_cat, v_b = _mla_project(plain, po["b_cq"], po["b_ckv"], po["b_kr"], q_norm, wq, kv_norm, wkv, rope_mla)
    o_b = _flash(q_cat, k_cat, v_b, bsz, seq, MLA_HEADS, MLA_QK_PAD, MLA_V, 0, 0, 0, hq ** -0.5)

    kmean = _moba_kmean(rot, bsz, seq, ro["c_k"])
    o_c = _flash(rot, rot, plain, bsz, seq, MOBA_HEADS, HEAD_DIM, HEAD_DIM, ro["c_q"], ro["c_k"], po["c_v"],
                 HEAD_DIM ** -0.5, kmean=kmean)

    o_d = _dsa_attention(rot, d_k, plain, iq, ik, iw, bsz, seq, ro["d_q"], po["d_v"])

    sizes = (DIL_OUT, MLA_HEADS * MLA_V, MOBA_HEADS * HEAD_DIM, DSA_HEADS * HEAD_DIM)
    projs, o = [], 0
    for sz in sizes:
        projs.append(w_branch[o:o + sz].astype(BF16))
        o += sz
    y = _branch_merge((o_a, o_b, o_c, o_d), projs, gates, d)
    return _matmul(y, w_out.astype(BF16), out_dtype=F32, tn_pref=512, epi="resid", xres=x, mod=mod, res_row=2, seq=seq)


def kernel(x, c, positions, w_ada, b_ada, ada_table_0, mix_norm_0, w_in_0, mla_q_norm_0, mla_q_up_0, mla_kv_norm_0, mla_kv_up_0, w_branch_0, w_out_0, ffn_norm_0, ffn_gate_0, ffn_up_0, ffn_down_0, ada_table_1, mix_norm_1, w_in_1, mla_q_norm_1, mla_q_up_1, mla_kv_norm_1, mla_kv_up_1, w_branch_1, w_out_1, ffn_norm_1, router_1, expert_gate_1, expert_up_1, expert_down_1, final_norm):
    bsz, seq, d = x.shape
    t = bsz * seq
    xf = x.reshape(t, d)
    mod0, mod1 = _ada_mod(c, w_ada, b_ada, ada_table_0, ada_table_1)

    pos_b = jnp.broadcast_to(positions.reshape(t, 1).astype(F32), (t, LANES))
    tables = (_rope_tables(pos_b, _rope_pattern(HEAD_DIM, ROT_DIM, LANES)),
              _rope_tables(pos_b, _rope_pattern(IDX_DIM, IDX_ROT, LANES)),
              _rope_tables(pos_b, _rope_pattern(IDX_DIM, IDX_ROT, IDX_DIM)),
              _rope_tables(pos_b, _rope_pattern(MLA_ROPE, MLA_ROPE, MLA_ROPE)))

    (h,) = _norm_mod(xf, mod0, mix_norm_0, seq, shift_row=0, scale_row=1)
    xf = _mixer(h, xf, mod0, bsz, seq, tables, w_in_0, mla_q_norm_0, mla_q_up_0, mla_kv_norm_0, mla_kv_up_0, w_branch_0, w_out_0)
    (h,) = _norm_mod(xf, mod0, ffn_norm_0, seq, shift_row=3, scale_row=4)
    y = _ffn_dense(h, ffn_gate_0.astype(BF16), ffn_up_0.astype(BF16), ffn_down_0.astype(BF16))

    xf, h = _norm_mod(xf, mod1, mix_norm_1, seq, shift_row=0, scale_row=1, y=y, res_mod=mod0, res_row=5)
    xf = _mixer(h, xf, mod1, bsz, seq, tables, w_in_1, mla_q_norm_1, mla_q_up_1, mla_kv_norm_1, mla_kv_up_1, w_branch_1, w_out_1)
    w_r = jnp.pad(router_1, ((0, 0), (0, LANES - N_EXPERTS))).astype(BF16)
    h, logits = _norm_mod(xf, mod1, ffn_norm_1, seq, shift_row=3, scale_row=4, w_router=w_r, h_dtype=U32)
    info, wts, cnt = _route(logits)
    e_flat = info[:, 0:2].reshape(2 * t)
    r_flat = info[:, 2:4].reshape(2 * t)
    counts = cnt[0, :N_EXPERTS].astype(I32)
    rows = min(MOE_ROWS, t)
    padded = (counts + rows - 1) // rows * rows
    pends = jnp.cumsum(padded).astype(I32)
    pstarts = pends - padded
    p_rows = (-(-2 * t // rows) + N_EXPERTS) * rows
    xbuf = _dispatch(h, e_flat, r_flat, pstarts, p_rows)
    ybuf = _ffn_experts(xbuf, pends, expert_gate_1.astype(BF16), expert_up_1.astype(BF16), expert_down_1.astype(BF16), rows)
    out = _combine_final(xf, ybuf, wts, mod1, final_norm, e_flat, r_flat, pstarts, seq, res_row=5)
    return out.reshape(bsz, seq, d)
```

```python
import functools

import numpy as np
import jax
import jax.numpy as jnp
from jax import lax
from jax.experimental import pallas as pl
from jax.experimental.pallas import tpu as pltpu

F32, BF16, I32, U32 = jnp.float32, jnp.bfloat16, jnp.int32, jnp.uint32

LANES = 128
SUBLANES = 8
VMEM_LIMIT_BYTES = 56 * 1024 * 1024

HEAD_DIM = 128
ROT_DIM = HEAD_DIM // 4
ROPE_THETA = 500000.0
NORM_EPS = 1e-6
MASK_VALUE = -1e30
DIL_GROUPS = ((128, 1), (512, 4), (2048, 16))
DIL_HEADS_PER_GROUP = 4
DIL_HEADS = len(DIL_GROUPS) * DIL_HEADS_PER_GROUP
DIL_WIDTH = DIL_HEADS * HEAD_DIM
DIL_OUT = DIL_HEADS_PER_GROUP * HEAD_DIM
MLA_HEADS = 8
MLA_Q_LORA = 1536
MLA_KV_LORA = 512
MLA_NOPE = 128
MLA_ROPE = 64
MLA_V = 128
MLA_QK_PAD = 256
MOBA_HEADS = 8
MOBA_BLOCK = 256
MOBA_TOPK = 3
DSA_HEADS = 8
DSA_TOPK = 256
IDX_HEADS = 32
IDX_DIM = 64
IDX_ROT = IDX_DIM // 4
N_BRANCH = 4
N_EXPERTS = 8
N_MOD = 6
INT_MIN = -2147483648

DIAG_ROWS = 512
DSA_TQ = 256
DSA_TK = 512
COUNT_ROWS = 64
MOE_ROWS = 512


def _cparams(n_axes):
    return pltpu.CompilerParams(dimension_semantics=("arbitrary",) * n_axes,
                                vmem_limit_bytes=VMEM_LIMIT_BYTES)


def _tile(n, pref, mult=LANES):
    if n <= pref:
        return n
    t = (pref // mult) * mult
    while t > mult and n % t:
        t -= mult
    assert n % t == 0, (n, pref, mult)
    return t


def _nt_dot(a, b):
    return lax.dot_general(a, b, (((1,), (1,)), ((), ())), preferred_element_type=F32)


def _pack_bf16_pairs(x):
    n = x.shape[1]
    bits = lax.bitcast_convert_type(x.astype(BF16).astype(F32), U32)
    return (bits[:, n // 2:] & jnp.uint32(0xFFFF0000)) | (bits[:, :n // 2] >> 16)


def _unpack_bf16_pairs(w):
    lo = lax.bitcast_convert_type(w << 16, F32).astype(BF16)
    hi = lax.bitcast_convert_type(w & jnp.uint32(0xFFFF0000), F32).astype(BF16)
    return lo, hi


def _rope_pattern(group, rot, active):
    half = rot // 2
    freqs = ROPE_THETA ** (-jnp.arange(half, dtype=F32) * (2.0 / rot))
    lane = np.arange(LANES)
    p = lane % group
    on = (lane < active) & (p < rot)
    idx = np.where(on, p % half, 0)
    freq = jnp.where(jnp.asarray(on), freqs[idx], 0.0)
    first = (on & (p < half)).astype(np.float32)
    second = (on & (p >= half)).astype(np.float32)
    pat = jnp.zeros((SUBLANES, LANES), F32)
    return pat.at[0].set(freq).at[1].set(jnp.asarray(first)).at[2].set(jnp.asarray(second))


def _rope_table_kernel(pos_ref, pat_ref, c_ref, sa_ref, sb_ref):
    ang = pos_ref[...] * pat_ref[0:1, :]
    c_ref[...] = jnp.cos(ang)
    sin = jnp.sin(ang)
    sa_ref[...] = -sin * pat_ref[1:2, :]
    sb_ref[...] = sin * pat_ref[2:3, :]


def _rope_tables(pos_b, pat):
    t = pos_b.shape[0]
    ts = _tile(t, 1024)
    spec = pl.BlockSpec((ts, LANES), lambda i: (i, 0))
    sds = jax.ShapeDtypeStruct((t, LANES), F32)
    return pl.pallas_call(
        _rope_table_kernel, grid=(t // ts,),
        in_specs=[spec, pl.BlockSpec((SUBLANES, LANES), lambda i: (0, 0))],
        out_specs=[spec, spec, spec], out_shape=[sds, sds, sds],
        compiler_params=_cparams(1))(pos_b, pat)


def _apply_rope(x, c, sa, sb, half):
    return x * c + pltpu.roll(x, LANES - half, 1) * sa + pltpu.roll(x, half, 1) * sb


def _ada_kernel(c_ref, w_ref, b_ref, t0_ref, t1_ref, o0_ref, o1_ref):
    c = c_ref[...]
    a = (c * jax.nn.sigmoid(c)).astype(BF16)
    acc = jnp.dot(a, w_ref[...].astype(BF16), preferred_element_type=F32) + b_ref[...]
    o0_ref[...] = acc + t0_ref[...]
    o1_ref[...] = acc + t1_ref[...]


def _ada_mod(c, w_ada, b_ada, table0, table1):
    b, d = c.shape
    n = w_ada.shape[1]
    bp = -(-b // SUBLANES) * SUBLANES
    cp = jnp.zeros((bp, d), F32).at[:b].set(c)
    tn = _tile(n, 512)
    row = pl.BlockSpec((1, tn), lambda j: (0, j))
    out = pl.BlockSpec((bp, tn), lambda j: (0, j))
    sds = jax.ShapeDtypeStruct((bp, n), F32)
    m0, m1 = pl.pallas_call(
        _ada_kernel, grid=(n // tn,),
        in_specs=[pl.BlockSpec((bp, d), lambda j: (0, 0)), pl.BlockSpec((d, tn), lambda j: (0, j)), row, row, row],
        out_specs=[out, out], out_shape=[sds, sds], compiler_params=_cparams(1),
    )(cp, w_ada, b_ada.reshape(1, n), table0.reshape(1, n), table1.reshape(1, n))

    def pack(m):
        m = m[:b].reshape(b, N_MOD, d)
        return jnp.concatenate([m, jnp.zeros((b, SUBLANES - N_MOD, d), F32)], axis=1)
    return pack(m0), pack(m1)


def _norm_kernel(*refs, res_row, shift_row, scale_row, with_logits, h_dtype):
    it = iter(refs)
    x_ref = next(it)
    y_ref, rmod_ref = (next(it), next(it)) if res_row is not None else (None, None)
    mod_ref, gain_ref = next(it), next(it)
    wr_ref = next(it) if with_logits else None
    xo_ref = next(it) if res_row is not None else None
    h_ref = next(it)
    lg_ref = next(it) if with_logits else None

    x = x_ref[...]
    if res_row is not None:
        x = x + rmod_ref[res_row:res_row + 1, :] * y_ref[...].astype(F32)
        xo_ref[...] = x
    y = x * lax.rsqrt(jnp.mean(x * x, axis=-1, keepdims=True) + NORM_EPS) * gain_ref[...]
    h = y * (1.0 + mod_ref[scale_row:scale_row + 1, :]) + mod_ref[shift_row:shift_row + 1, :]
    if h_dtype == U32:
        h_ref[...] = _pack_bf16_pairs(h)
    else:
        h_ref[...] = h.astype(h_dtype)
    if with_logits:
        lg_ref[...] = jnp.dot(h.astype(BF16), wr_ref[...], preferred_element_type=F32)


def _norm_mod(x, mod, gain, seq, *, shift_row, scale_row, y=None, res_mod=None, res_row=None, w_router=None,
              h_dtype=BF16):
    t, d = x.shape
    ts = _tile(seq, 256, SUBLANES)
    per_b = seq // ts
    blk = pl.BlockSpec((ts, d), lambda i: (i, 0))
    mod_spec = pl.BlockSpec((None, SUBLANES, d), lambda i: (i // per_b, 0, 0))
    in_specs, args = [blk], [x]
    if res_row is not None:
        in_specs += [blk, mod_spec]
        args += [y, res_mod]
    in_specs += [mod_spec, pl.BlockSpec((1, d), lambda i: (0, 0))]
    args += [mod, gain.reshape(1, d)]
    if w_router is not None:
        in_specs.append(pl.BlockSpec((d, LANES), lambda i: (0, 0)))
        args.append(w_router)
    out_specs, out_shape = [], []
    if res_row is not None:
        out_specs.append(blk)
        out_shape.append(jax.ShapeDtypeStruct((t, d), F32))
    hd = d // 2 if h_dtype == U32 else d
    out_specs.append(pl.BlockSpec((ts, hd), lambda i: (i, 0)))
    out_shape.append(jax.ShapeDtypeStruct((t, hd), h_dtype))
    if w_router is not None:
        out_specs.append(pl.BlockSpec((ts, LANES), lambda i: (i, 0)))
        out_shape.append(jax.ShapeDtypeStruct((t, LANES), F32))
    kern = functools.partial(_norm_kernel, res_row=res_row, shift_row=shift_row, scale_row=scale_row,
                             with_logits=w_router is not None, h_dtype=h_dtype)
    return pl.pallas_call(kern, grid=(t // ts,), in_specs=in_specs, out_specs=out_specs, out_shape=out_shape,
                          compiler_params=_cparams(1))(*args)


def _mm_kernel(*refs, epi, half, res_row):
    a_ref, w_ref = refs[0], refs[1]
    o_ref = refs[-1]
    acc = jnp.dot(a_ref[...], w_ref[...], preferred_element_type=F32)
    if epi == "plain":
        o_ref[...] = acc.astype(o_ref.dtype)
    elif epi == "sigmoid":
        o_ref[...] = jax.nn.sigmoid(acc).astype(o_ref.dtype)
    elif epi == "rope":
        c, sa, sb = refs[2][...], refs[3][...], refs[4][...]
        for ch in range(acc.shape[1] // LANES):
            sl = slice(ch * LANES, (ch + 1) * LANES)
            o_ref[:, sl] = _apply_rope(acc[:, sl], c, sa, sb, half).astype(o_ref.dtype)
    elif epi == "resid":
        xres_ref, mod_ref = refs[2], refs[3]
        o_ref[...] = xres_ref[...] + mod_ref[res_row:res_row + 1, :] * acc
    else:
        raise ValueError(epi)


def _matmul(a, w, *, out_dtype, tn_pref, epi="plain", rope=None, half=0, xres=None, mod=None, res_row=None, seq=None):
    m, k = a.shape
    n = w.shape[1]
    tm = _tile(seq if seq is not None else m, 1024, SUBLANES)
    tn = _tile(n, tn_pref)
    in_specs = [pl.BlockSpec((tm, k), lambda i, j: (i, 0)), pl.BlockSpec((k, tn), lambda i, j: (0, j))]
    args = [a, w]
    if epi == "rope":
        in_specs += [pl.BlockSpec((tm, LANES), lambda i, j: (i, 0))] * 3
        args += list(rope)
    if epi == "resid":
        per_b = seq // tm
        in_specs += [pl.BlockSpec((tm, tn), lambda i, j: (i, j)),
                     pl.BlockSpec((None, SUBLANES, tn), lambda i, j: (i // per_b, 0, j))]
        args += [xres, mod]
    kern = functools.partial(_mm_kernel, epi=epi, half=half, res_row=res_row)
    return pl.pallas_call(
        kern, grid=(m // tm, n // tn), in_specs=in_specs,
        out_specs=pl.BlockSpec((tm, tn), lambda i, j: (i, j)),
        out_shape=jax.ShapeDtypeStruct((m, n), out_dtype), compiler_params=_cparams(2))(*args)


def _dil_kernel(q_ref, kp_ref, ko_ref, vp_ref, vo_ref, o_ref, lse_ref, *, scale):
    i = pl.program_id(2)
    w = q_ref.shape[0]
    row = lax.broadcasted_iota(I32, (w, w), 0)
    col = lax.broadcasted_iota(I32, (w, w), 1)
    for h in range(DIL_HEADS_PER_GROUP):
        sl = slice(h * HEAD_DIM, (h + 1) * HEAD_DIM)
        q = q_ref[:, sl]
        s_o = jnp.where(col <= row, _nt_dot(q, ko_ref[:, sl]) * scale, MASK_VALUE)
        s_p = jnp.where(col >= row, _nt_dot(q, kp_ref[:, sl]) * scale, MASK_VALUE)
        s_p = jnp.where(i > 0, s_p, MASK_VALUE)
        m = jnp.maximum(jnp.max(s_o, axis=1, keepdims=True), jnp.max(s_p, axis=1, keepdims=True))
        p_o = jnp.exp(s_o - m)
        p_p = jnp.exp(s_p - m)
        l = jnp.sum(p_o, axis=1, keepdims=True) + jnp.sum(p_p, axis=1, keepdims=True)
        o = (jnp.dot(p_o.astype(BF16), vo_ref[:, sl], preferred_element_type=F32)
             + jnp.dot(p_p.astype(BF16), vp_ref[:, sl], preferred_element_type=F32))
        o_ref[:, sl] = o / l
        lse_ref[:, sl] = jnp.broadcast_to(m + jnp.log(l), o.shape)


def _dilated_group(qk, v, bsz, seq, g):
    window, dil = DIL_GROUPS[g]
    wsub = window // dil
    n = seq // dil
    assert wsub == HEAD_DIM and n % wsub == 0
    nb = n // wsub
    qk_v = qk.reshape(bsz, n, dil * 2 * DIL_OUT)
    v_v = v.reshape(bsz, n, dil * DIL_OUT)
    blk = (None, wsub, DIL_OUT)
    q_spec = pl.BlockSpec(blk, lambda b, r, i: (b, i, 2 * r))
    k_own = pl.BlockSpec(blk, lambda b, r, i: (b, i, 2 * r + 1))
    k_prev = pl.BlockSpec(blk, lambda b, r, i: (b, jnp.maximum(i - 1, 0), 2 * r + 1))
    v_own = pl.BlockSpec(blk, lambda b, r, i: (b, i, r))
    v_prev = pl.BlockSpec(blk, lambda b, r, i: (b, jnp.maximum(i - 1, 0), r))
    sds = jax.ShapeDtypeStruct((bsz, n, dil * DIL_OUT), F32)
    o, lse = pl.pallas_call(
        functools.partial(_dil_kernel, scale=HEAD_DIM ** -0.5), grid=(bsz, dil, nb),
        in_specs=[q_spec, k_prev, k_own, v_prev, v_own],
        out_specs=[v_own, v_own], out_shape=[sds, sds], compiler_params=_cparams(3),
    )(qk_v, qk_v, qk_v, v_v, v_v)
    return o.reshape(bsz * seq, DIL_OUT), lse.reshape(bsz * seq, DIL_OUT)


def _dil_combine_kernel(o0, o1, o2, l0, l1, l2, out_ref):
    a, b, c = l0[...], l1[...], l2[...]
    m = jnp.maximum(jnp.maximum(a, b), c)
    ea, eb, ec = jnp.exp(a - m), jnp.exp(b - m), jnp.exp(c - m)
    out_ref[...] = ((ea * o0[...] + eb * o1[...] + ec * o2[...]) / (ea + eb + ec)).astype(out_ref.dtype)


def _dilated_mixture(qks, vs, bsz, seq):
    outs, lses = zip(*[_dilated_group(qks[g], vs[g], bsz, seq, g) for g in range(len(DIL_GROUPS))])
    t = bsz * seq
    ts = _tile(t, 1024, SUBLANES)
    spec = pl.BlockSpec((ts, DIL_OUT), lambda i: (i, 0))
    return pl.pallas_call(
        _dil_combine_kernel, grid=(t // ts,), in_specs=[spec] * 6, out_specs=spec,
        out_shape=jax.ShapeDtypeStruct((t, DIL_OUT), BF16), compiler_params=_cparams(1))(*outs, *lses)


def _rms(x, gain):
    return x * lax.rsqrt(jnp.mean(x * x, axis=-1, keepdims=True) + NORM_EPS) * gain


def _mla_q_kernel(cq_ref, gain_ref, w_ref, c_ref, sa_ref, sb_ref, o_ref):
    cqn = _rms(cq_ref[...].astype(F32), gain_ref[...]).astype(BF16)
    acc = jnp.dot(cqn, w_ref[...], preferred_element_type=F32)
    c, sa, sb = c_ref[...], sa_ref[...], sb_ref[...]
    for ch in range(acc.shape[1] // LANES):
        sl = slice(ch * LANES, (ch + 1) * LANES)
        x = acc[:, sl]
        if ch % 2 == 1:
            x = _apply_rope(x, c, sa, sb, MLA_ROPE // 2)
        o_ref[:, sl] = x.astype(o_ref.dtype)


def _mla_kv_kernel(ckv_ref, gain_ref, w_ref, kr_ref, c_ref, sa_ref, sb_ref, k_ref, v_ref):
    ckvn = _rms(ckv_ref[...].astype(F32), gain_ref[...]).astype(BF16)
    acc = jnp.dot(ckvn, w_ref[...], preferred_element_type=F32)
    kpe = _apply_rope(kr_ref[...].astype(F32), c_ref[...], sa_ref[...], sb_ref[...], MLA_ROPE // 2).astype(k_ref.dtype)
    hw = MLA_HEADS * MLA_NOPE
    for h in range(MLA_HEADS):
        k_ref[:, h * MLA_QK_PAD:h * MLA_QK_PAD + MLA_NOPE] = acc[:, h * MLA_NOPE:(h + 1) * MLA_NOPE].astype(k_ref.dtype)
        k_ref[:, h * MLA_QK_PAD + MLA_NOPE:(h + 1) * MLA_QK_PAD] = kpe
    v_ref[...] = acc[:, hw:].astype(v_ref.dtype)


def _mla_project(plain, cq_col, ckv_col, kr_col, q_gain, wq, kv_gain, wkv, rope):
    t = plain.shape[0]
    ts = _tile(t, 512, SUBLANES)
    tab = pl.BlockSpec((ts, LANES), lambda i: (i, 0))
    nq = MLA_HEADS * MLA_QK_PAD
    q_cat = pl.pallas_call(
        _mla_q_kernel, grid=(t // ts,),
        in_specs=[pl.BlockSpec((ts, MLA_Q_LORA), lambda i: (i, cq_col * LANES // MLA_Q_LORA)),
                  pl.BlockSpec((1, MLA_Q_LORA), lambda i: (0, 0)),
                  pl.BlockSpec((MLA_Q_LORA, nq), lambda i: (0, 0)), tab, tab, tab],
        out_specs=pl.BlockSpec((ts, nq), lambda i: (i, 0)),
        out_shape=jax.ShapeDtypeStruct((t, nq), BF16), compiler_params=_cparams(1),
    )(plain, q_gain.reshape(1, -1), wq, *rope)
    nv = MLA_HEADS * MLA_V
    k_cat, v = pl.pallas_call(
        _mla_kv_kernel, grid=(t // ts,),
        in_specs=[pl.BlockSpec((ts, MLA_KV_LORA), lambda i: (i, ckv_col * LANES // MLA_KV_LORA)),
                  pl.BlockSpec((1, MLA_KV_LORA), lambda i: (0, 0)),
                  pl.BlockSpec((MLA_KV_LORA, MLA_HEADS * MLA_NOPE + nv), lambda i: (0, 0)),
                  pl.BlockSpec((ts, LANES), lambda i: (i, kr_col)), tab, tab, tab],
        out_specs=[pl.BlockSpec((ts, nq), lambda i: (i, 0)), pl.BlockSpec((ts, nv), lambda i: (i, 0))],
        out_shape=[jax.ShapeDtypeStruct((t, nq), BF16), jax.ShapeDtypeStruct((t, nv), BF16)],
        compiler_params=_cparams(1),
    )(plain, kv_gain.reshape(1, -1), wkv, plain, *rope)
    return q_cat, k_cat, v


def _softmax_step(s, v, m_sc, l_sc, acc_sc, rows):
    tiles = [s[:, j * LANES:(j + 1) * LANES] for j in range(s.shape[1] // LANES)]
    m_prev = m_sc[rows, :]
    m_new = jnp.maximum(m_prev, jnp.max(functools.reduce(jnp.maximum, tiles), axis=1, keepdims=True))
    alpha = jnp.exp(m_prev - m_new)
    ps = [jnp.exp(x - m_new) for x in tiles]
    l_sc[rows, :] = alpha * l_sc[rows, :] + jnp.sum(functools.reduce(jnp.add, ps), axis=1, keepdims=True)
    p = jnp.concatenate([x.astype(BF16) for x in ps], axis=1)
    acc_sc[rows, :] = alpha * acc_sc[rows, :] + jnp.dot(p, v, preferred_element_type=F32)
    m_sc[rows, :] = m_new


def _flash_kernel(*refs, scale, moba):
    if moba:
        qi_ref, st_ref, q_ref, k_ref, v_ref, km_ref, o_ref, m_sc, l_sc, acc_sc, sel_sc = refs
    else:
        qi_ref, st_ref, q_ref, k_ref, v_ref, o_ref, m_sc, l_sc, acc_sc = refs
    qi, step = qi_ref[pl.program_id(2)], st_ref[pl.program_id(2)]
    kv = qi - step
    tq, tk = q_ref.shape[0], k_ref.shape[0]

    @pl.when(step == 0)
    def _init():
        m_sc[...] = jnp.full(m_sc.shape, MASK_VALUE, F32)
        l_sc[...] = jnp.zeros(l_sc.shape, F32)
        acc_sc[...] = jnp.zeros(acc_sc.shape, F32)
        if moba:
            lane = lax.broadcasted_iota(I32, (tq, LANES), 1)
            own = (qi * tq + lax.broadcasted_iota(I32, (tq, LANES), 0)) // MOBA_BLOCK
            valid = lane < own
            g = jnp.where(valid, _nt_dot(q_ref[...], km_ref[...]), MASK_VALUE)
            sel = jnp.zeros((tq, LANES), F32)
            lane_f = lane.astype(F32)
            for _ in range(MOBA_TOPK):
                mx = jnp.max(g, axis=1, keepdims=True)
                first = jnp.min(jnp.where(g == mx, lane_f, float(LANES)), axis=1, keepdims=True)
                pick = lane_f == first
                sel = jnp.where(pick, 1.0, sel)
                g = jnp.where(pick, -jnp.inf, g)
            sel_sc[...] = jnp.where(valid, sel, 0.0)

    def block_flags(rows, n_rows, n_keys):
        lane = lax.broadcasted_iota(I32, (n_rows, LANES), 1)
        sel = sel_sc[rows, :]
        nsub = tk // MOBA_BLOCK
        flags = []
        for c in range(n_keys // MOBA_BLOCK):
            f = jnp.sum(jnp.where(lane == kv * nsub + c, sel, 0.0), axis=1, keepdims=True)
            flags.append(jnp.broadcast_to(f, (n_rows, MOBA_BLOCK)))
        return flags[0] if len(flags) == 1 else jnp.concatenate(flags, axis=1)

    @pl.when(step == 0)
    def _diagonal():
        dc = min(DIAG_ROWS, tq)
        for r0 in range(0, tq, dc):
            rows, kw = slice(r0, r0 + dc), r0 + dc
            s = _nt_dot(q_ref[rows, :], k_ref[0:kw, :]) * scale
            row = r0 + lax.broadcasted_iota(I32, (dc, kw), 0)
            col = lax.broadcasted_iota(I32, (dc, kw), 1)
            ok = col <= row
            if moba:
                same = (col // MOBA_BLOCK) == (row // MOBA_BLOCK)
                ok = jnp.logical_and(ok, jnp.logical_or(same, block_flags(rows, dc, kw) > 0.0))
            _softmax_step(jnp.where(ok, s, MASK_VALUE), v_ref[0:kw, :], m_sc, l_sc, acc_sc, rows)

    @pl.when(step > 0)
    def _past():
        s = _nt_dot(q_ref[...], k_ref[...]) * scale
        if moba:
            s = jnp.where(block_flags(slice(None), tq, tk) > 0.0, s, MASK_VALUE)
        _softmax_step(s, v_ref[...], m_sc, l_sc, acc_sc, slice(None))

    @pl.when(step == qi)
    def _fin():
        o_ref[...] = (acc_sc[...] / l_sc[:, 0:1]).astype(o_ref.dtype)


def _flash(q, k, v, bsz, seq, heads, dq, dv, q_col, k_col, v_col, scale, kmean=None):
    t = bsz * seq
    tq = _tile(seq, 1024, MOBA_BLOCK)
    nq = seq // tq
    pairs = [(i, s) for i in range(nq) for s in range(i + 1)]
    qi_arr = jnp.asarray([p[0] for p in pairs], I32)
    st_arr = jnp.asarray([p[1] for p in pairs], I32)
    in_specs = [pl.BlockSpec((tq, dq), lambda b, h, p, qi, st: (b * nq + qi[p], q_col + h)),
                pl.BlockSpec((tq, dq), lambda b, h, p, qi, st: (b * nq + qi[p] - st[p], k_col + h)),
                pl.BlockSpec((tq, dv), lambda b, h, p, qi, st: (b * nq + qi[p] - st[p], v_col + h))]
    args = [q, k, v]
    scratch = [pltpu.VMEM((tq, LANES), F32), pltpu.VMEM((tq, LANES), F32), pltpu.VMEM((tq, dv), F32)]
    if kmean is not None:
        in_specs.append(pl.BlockSpec((None, None, LANES, HEAD_DIM), lambda b, h, p, qi, st: (b, h, 0, 0)))
        args.append(kmean)
        scratch.append(pltpu.VMEM((tq, LANES), F32))
    grid_spec = pltpu.PrefetchScalarGridSpec(
        num_scalar_prefetch=2, grid=(bsz, heads, len(pairs)), in_specs=in_specs,
        out_specs=pl.BlockSpec((tq, dv), lambda b, h, p, qi, st: (b * nq + qi[p], h)),
        scratch_shapes=scratch)
    return pl.pallas_call(
        functools.partial(_flash_kernel, scale=scale, moba=kmean is not None), grid_spec=grid_spec,
        out_shape=jax.ShapeDtypeStruct((t, heads * dv), BF16), compiler_params=_cparams(3))(qi_arr, st_arr, *args)


def _kmean_kernel(k_ref, o_ref):
    s = k_ref.shape[0]
    nb = s // MOBA_BLOCK
    k = k_ref[...].astype(F32).reshape(nb, MOBA_BLOCK, HEAD_DIM)
    mean = jnp.sum(k, axis=1) * (1.0 / MOBA_BLOCK)
    full = jnp.concatenate([mean, jnp.zeros((LANES - nb, HEAD_DIM), F32)], axis=0)
    o_ref[...] = full.astype(o_ref.dtype)


def _moba_kmean(rot, bsz, seq, k_col):
    assert seq % MOBA_BLOCK == 0 and seq // MOBA_BLOCK <= LANES
    return pl.pallas_call(
        _kmean_kernel, grid=(bsz, MOBA_HEADS),
        in_specs=[pl.BlockSpec((seq, HEAD_DIM), lambda b, h: (b, k_col + h))],
        out_specs=pl.BlockSpec((None, None, LANES, HEAD_DIM), lambda b, h: (b, h, 0, 0)),
        out_shape=jax.ShapeDtypeStruct((bsz, MOBA_HEADS, LANES, HEAD_DIM), BF16),
        compiler_params=_cparams(2))(rot)


def _dsa_kernel(q_ref, iq_ref, iw_ref, ikbd_ref, k_ref, v_ref, o_ref, key_sc, cand_sc, cnt_sc, qs_sc, m_sc, l_sc, acc_sc, *, n_keep, scale,
                wscale):
    c = pl.program_id(1)
    tq, tk = DSA_TQ, DSA_TK
    nkt = (c * tq) // tk + 1
    rowg = c * tq + lax.broadcasted_iota(I32, (tq, tk), 0)
    col = lax.broadcasted_iota(I32, (tq, tk), 1)
    w = iw_ref[...] * wscale

    def score_tile(kt, carry):
        ikb = ikbd_ref[kt]
        sc = jnp.zeros((tq, tk), F32)
        for j in range(IDX_HEADS // 2):
            rel = jnp.maximum(_nt_dot(iq_ref[:, j * LANES:(j + 1) * LANES], ikb), 0.0)
            sc = sc + rel[:, :tk] * w[:, 2 * j:2 * j + 1] + rel[:, tk:] * w[:, 2 * j + 1:2 * j + 2]
        bits = lax.bitcast_convert_type(sc, I32)
        key = bits ^ ((bits >> 31) & 0x7FFFFFFF)
        key_sc[kt] = jnp.where(kt * tk + col <= rowg, key, INT_MIN)
        return carry
    lax.fori_loop(0, nkt, score_tile, 0)

    def count_ge(cand):
        cand_sc[...] = cand
        cnt_sc[...] = jnp.zeros(cnt_sc.shape, F32)

        def body(kt, carry):
            for r0 in range(0, tq, COUNT_ROWS):
                rows = slice(r0, r0 + COUNT_ROWS)
                c = cand_sc[rows, :]
                a = cnt_sc[rows, :]
                for j in range(tk // LANES):
                    a = a + jnp.where(key_sc[kt, rows, j * LANES:(j + 1) * LANES] >= c, 1.0, 0.0)
                cnt_sc[rows, :] = a
            return carry
        lax.fori_loop(0, nkt, body, 0)
        return jnp.broadcast_to(jnp.sum(cnt_sc[...], axis=1, keepdims=True), (tq, LANES))

    keep = jnp.float32(n_keep)
    thr = jnp.where(count_ge(jnp.zeros((tq, LANES), I32)) >= keep, 0, INT_MIN).astype(I32)

    def search(it, thr):
        cand = thr | (jnp.int32(1) << (30 - it))
        return jnp.where(count_ge(cand) >= keep, cand, thr)
    thr = lax.fori_loop(0, 31, search, thr)
    thr = jnp.maximum(thr, INT_MIN + 1)

    for h in range(DSA_HEADS):
        qs_sc[h * tq:(h + 1) * tq, :] = q_ref[:, h * HEAD_DIM:(h + 1) * HEAD_DIM]
    m_sc[...] = jnp.full(m_sc.shape, MASK_VALUE, F32)
    l_sc[...] = jnp.zeros(l_sc.shape, F32)
    acc_sc[...] = jnp.zeros(acc_sc.shape, F32)

    def attend(kt, carry):
        sel = jnp.concatenate([key_sc[kt, :, j * LANES:(j + 1) * LANES] >= thr for j in range(tk // LANES)], axis=1)
        ks = k_ref[pl.ds(pl.multiple_of(kt * tk, tk), tk), :]
        vs = v_ref[pl.ds(pl.multiple_of(kt * tk, tk), tk), :]
        s = _nt_dot(qs_sc[...], ks) * scale
        s = jnp.concatenate([jnp.where(sel, s[h * tq:(h + 1) * tq, :], MASK_VALUE) for h in range(DSA_HEADS)], axis=0)
        _softmax_step(s, vs, m_sc, l_sc, acc_sc, slice(None))
        return carry
    lax.fori_loop(0, nkt, attend, 0)

    for h in range(DSA_HEADS):
        rows = slice(h * tq, (h + 1) * tq)
        o_ref[:, h * HEAD_DIM:(h + 1) * HEAD_DIM] = (acc_sc[rows, :] / l_sc[rows, 0:1]).astype(o_ref.dtype)


def _dsa_attention(rot, k, plain, iq, ik, iw, bsz, seq, q_col, v_col):
    t = bsz * seq
    assert seq % DSA_TK == 0
    nkt = seq // DSA_TK
    nch = seq // DSA_TQ
    ika = ik.astype(BF16).reshape(bsz, nkt, DSA_TK, LANES)
    ikbd = jnp.concatenate([ika, jnp.roll(ika, IDX_DIM, axis=-1)], axis=2)
    n_keep = min(DSA_TOPK, seq // 4)
    qw = DSA_HEADS * HEAD_DIM
    iqw = IDX_HEADS * IDX_DIM
    kern = functools.partial(_dsa_kernel, n_keep=n_keep, scale=HEAD_DIM ** -0.5,
                             wscale=(IDX_HEADS ** -0.5) * (IDX_DIM ** -0.5))
    return pl.pallas_call(
        kern, grid=(bsz, nch),
        in_specs=[pl.BlockSpec((DSA_TQ, qw), lambda b, c: (b * nch + c, q_col * HEAD_DIM // qw)),
                  pl.BlockSpec((DSA_TQ, iqw), lambda b, c: (b * nch + c, 0)),
                  pl.BlockSpec((DSA_TQ, LANES), lambda b, c: (b * nch + c, 0)),
                  pl.BlockSpec((None, nkt, 2 * DSA_TK, LANES), lambda b, c: (b, 0, 0, 0)),
                  pl.BlockSpec((seq, HEAD_DIM), lambda b, c: (b, 0)),
                  pl.BlockSpec((seq, HEAD_DIM), lambda b, c: (b, v_col))],
        out_specs=pl.BlockSpec((DSA_TQ, qw), lambda b, c: (b * nch + c, 0)),
        out_shape=jax.ShapeDtypeStruct((t, qw), BF16),
        scratch_shapes=[pltpu.VMEM((nkt, DSA_TQ, DSA_TK), I32), pltpu.VMEM((DSA_TQ, LANES), I32),
                        pltpu.VMEM((DSA_TQ, LANES), F32), pltpu.VMEM((DSA_HEADS * DSA_TQ, HEAD_DIM), BF16),
                        pltpu.VMEM((DSA_HEADS * DSA_TQ, LANES), F32), pltpu.VMEM((DSA_HEADS * DSA_TQ, LANES), F32),
                        pltpu.VMEM((DSA_HEADS * DSA_TQ, HEAD_DIM), F32)],
        compiler_params=_cparams(2))(rot, iq, iw, ikbd, k, plain)


def _branch_kernel(oa, ob, oc, od, pa, pb, pc, pd, ga, gb, gc, gd, y_ref):
    y = None
    for o, p, g in ((oa, pa, ga), (ob, pb, gb), (oc, pc, gc), (od, pd, gd)):
        term = g[...].astype(F32) * jnp.dot(o[...], p[...], preferred_element_type=F32)
        y = term if y is None else y + term
    y_ref[...] = y.astype(y_ref.dtype)


def _branch_merge(outs, projs, gates, d):
    t = outs[0].shape[0]
    tm = _tile(t, 1024, SUBLANES)
    tn = _tile(d, 512)
    nj = d // tn
    in_specs = [pl.BlockSpec((tm, o.shape[1]), lambda i, j: (i, 0)) for o in outs]
    in_specs += [pl.BlockSpec((p.shape[0], tn), lambda i, j: (0, j)) for p in projs]
    in_specs += [pl.BlockSpec((tm, tn), functools.partial(lambda i, j, k: (i, k * nj + j), k=k)) for k in range(N_BRANCH)]
    return pl.pallas_call(
        _branch_kernel, grid=(t // tm, nj), in_specs=in_specs,
        out_specs=pl.BlockSpec((tm, tn), lambda i, j: (i, j)),
        out_shape=jax.ShapeDtypeStruct((t, d), BF16), compiler_params=_cparams(2),
    )(*outs, *projs, gates, gates, gates, gates)


def _glu_kernel(*refs, n_prefetch):
    pends_ref = refs[0] if n_prefetch else None
    x_ref, wg_ref, wu_ref, o_ref = refs[n_prefetch:n_prefetch + 4]

    def compute():
        if x_ref.dtype == U32:
            xb_sc = refs[n_prefetch + 4]

            @pl.when(pl.program_id(1) == 0)
            def _():
                lo, hi = _unpack_bf16_pairs(x_ref[...])
                xb_sc[:, :lo.shape[1]] = lo
                xb_sc[:, lo.shape[1]:] = hi
            x = xb_sc[...]
        else:
            x = x_ref[...]
        g = jnp.dot(x, wg_ref[...], preferred_element_type=F32)
        u = jnp.dot(x, wu_ref[...], preferred_element_type=F32)
        o_ref[...] = (g * jax.nn.sigmoid(g) * u).astype(o_ref.dtype)

    if n_prefetch:
        used = pl.program_id(0) * x_ref.shape[0] < pends_ref[N_EXPERTS - 1]
        pl.when(used)(compute)

        @pl.when(jnp.logical_not(used))
        def _():
            o_ref[...] = jnp.zeros(o_ref.shape, o_ref.dtype)
    else:
        compute()


def _mmk_kernel(a_ref, w_ref, o_ref, acc_sc):
    k = pl.program_id(2)
    part = jnp.dot(a_ref[...], w_ref[...], preferred_element_type=F32)

    @pl.when(k == 0)
    def _():
        acc_sc[...] = part

    @pl.when(k > 0)
    def _():
        acc_sc[...] += part

    @pl.when(k == pl.num_programs(2) - 1)
    def _():
        o_ref[...] = acc_sc[...].astype(o_ref.dtype)


def _ffn_dense(h, wg, wu, wd):
    t, d = h.shape
    ff = wg.shape[1]
    tm = _tile(t, 1024, SUBLANES)
    tf = _tile(ff, 512)
    a = pl.pallas_call(
        functools.partial(_glu_kernel, n_prefetch=0), grid=(t // tm, ff // tf),
        in_specs=[pl.BlockSpec((tm, d), lambda i, f: (i, 0)), pl.BlockSpec((d, tf), lambda i, f: (0, f)),
                  pl.BlockSpec((d, tf), lambda i, f: (0, f))],
        out_specs=pl.BlockSpec((tm, tf), lambda i, f: (i, f)),
        out_shape=jax.ShapeDtypeStruct((t, ff), BF16), compiler_params=_cparams(2))(h, wg, wu)
    tn = _tile(d, 2048)
    tk = _tile(ff, 2048)
    return pl.pallas_call(
        _mmk_kernel, grid=(t // tm, d // tn, ff // tk),
        in_specs=[pl.BlockSpec((tm, tk), lambda i, j, k: (i, k)), pl.BlockSpec((tk, tn), lambda i, j, k: (k, j))],
        out_specs=pl.BlockSpec((tm, tn), lambda i, j, k: (i, j)),
        out_shape=jax.ShapeDtypeStruct((t, d), BF16),
        scratch_shapes=[pltpu.VMEM((tm, tn), F32)], compiler_params=_cparams(3))(a, wd)


def _down_experts_kernel(pends_ref, a_ref, w_ref, o_ref):
    used = pl.program_id(0) * a_ref.shape[0] < pends_ref[N_EXPERTS - 1]

    @pl.when(used)
    def _():
        o_ref[...] = jnp.dot(a_ref[...], w_ref[...], preferred_element_type=F32)

    @pl.when(jnp.logical_not(used))
    def _():
        o_ref[...] = jnp.zeros(o_ref.shape, o_ref.dtype)


def _ffn_experts(xbuf, pends, wg, wu, wd, rows):
    p = xbuf.shape[0]
    d, ff = wg.shape[1], wg.shape[2]
    tf = _tile(ff, 512)

    def expert(i, pends_ref):
        e = jnp.int32(0)
        for j in range(N_EXPERTS - 1):
            e = e + (i * rows >= pends_ref[j]).astype(I32)
        return e

    a = pl.pallas_call(
        functools.partial(_glu_kernel, n_prefetch=1),
        grid_spec=pltpu.PrefetchScalarGridSpec(
            num_scalar_prefetch=1, grid=(p // rows, ff // tf),
            in_specs=[pl.BlockSpec((rows, d // 2), lambda i, f, pe: (i, 0)),
                      pl.BlockSpec((None, d, tf), lambda i, f, pe: (expert(i, pe), 0, f)),
                      pl.BlockSpec((None, d, tf), lambda i, f, pe: (expert(i, pe), 0, f))],
            out_specs=pl.BlockSpec((rows, tf), lambda i, f, pe: (i, f)),
            scratch_shapes=[pltpu.VMEM((rows, d), BF16)]),
        out_shape=jax.ShapeDtypeStruct((p, ff), BF16), compiler_params=_cparams(2))(pends, xbuf, wg, wu)
    tn = _tile(d, 1024)
    return pl.pallas_call(
        _down_experts_kernel,
        grid_spec=pltpu.PrefetchScalarGridSpec(
            num_scalar_prefetch=1, grid=(p // rows, d // tn),
            in_specs=[pl.BlockSpec((rows, ff), lambda i, j, pe: (i, 0)),
                      pl.BlockSpec((None, ff, tn), lambda i, j, pe: (expert(i, pe), 0, j))],
            out_specs=pl.BlockSpec((rows, tn), lambda i, j, pe: (i, j))),
        out_shape=jax.ShapeDtypeStruct((p, d), F32), compiler_params=_cparams(2))(pends, a, wd)


def _route_kernel(lg_ref, info_ref, w_ref, cnt_ref, carry_sc):
    i = pl.program_id(0)

    @pl.when(i == 0)
    def _():
        carry_sc[...] = jnp.zeros(carry_sc.shape, F32)

    tr = lg_ref.shape[0]
    lane = lax.broadcasted_iota(I32, (tr, LANES), 1)
    lg = jnp.where(lane < N_EXPERTS, lg_ref[...], -jnp.inf)
    m1 = jnp.max(lg, axis=1, keepdims=True)
    i1 = jnp.min(jnp.where(lg == m1, lane, LANES), axis=1, keepdims=True)
    lg2 = jnp.where(lane == i1, -jnp.inf, lg)
    m2 = jnp.max(lg2, axis=1, keepdims=True)
    i2 = jnp.min(jnp.where(lg2 == m2, lane, LANES), axis=1, keepdims=True)
    e = jnp.exp(m2 - m1)
    w1 = 1.0 / (1.0 + e)
    w2 = e * w1
    oh = jnp.where(jnp.logical_or(lane == i1, lane == i2), 1.0, 0.0)
    tri = jnp.where(lax.broadcasted_iota(I32, (tr, tr), 1) < lax.broadcasted_iota(I32, (tr, tr), 0), 1.0, 0.0)
    before = jnp.dot(tri.astype(BF16), oh.astype(BF16), preferred_element_type=F32) + carry_sc[0:1, :]
    r1 = jnp.sum(jnp.where(lane == i1, before, 0.0), axis=1, keepdims=True).astype(I32)
    r2 = jnp.sum(jnp.where(lane == i2, before, 0.0), axis=1, keepdims=True).astype(I32)
    carry_sc[0:1, :] = carry_sc[0:1, :] + jnp.sum(oh, axis=0, keepdims=True)
    info_ref[...] = jnp.where(lane == 0, i1, jnp.where(lane == 1, i2, jnp.where(lane == 2, r1, jnp.where(lane == 3, r2, 0))))
    w_ref[...] = jnp.where(lane == 0, w1, jnp.where(lane == 1, w2, 0.0))
    cnt_ref[...] = carry_sc[...]


def _route(logits):
    t = logits.shape[0]
    tr = _tile(t, 256, SUBLANES)
    blk = pl.BlockSpec((tr, LANES), lambda i: (i, 0))
    return pl.pallas_call(
        _route_kernel, grid=(t // tr,), in_specs=[blk],
        out_specs=[blk, blk, pl.BlockSpec((SUBLANES, LANES), lambda i: (0, 0))],
        out_shape=[jax.ShapeDtypeStruct((t, LANES), I32), jax.ShapeDtypeStruct((t, LANES), F32),
                   jax.ShapeDtypeStruct((SUBLANES, LANES), F32)],
        scratch_shapes=[pltpu.VMEM((SUBLANES, LANES), F32)], compiler_params=_cparams(1))(logits)


def _dispatch_kernel(e_ref, r_ref, ps_ref, h_ref, xz_hbm, xbuf_hbm, sem):
    del xz_hbm
    tr = h_ref.shape[0]
    base = pl.program_id(0) * tr

    def copy(r, k):
        a = (base + r) * 2 + k
        dst = ps_ref[e_ref[a]] + r_ref[a]
        return pltpu.make_async_copy(h_ref.at[pl.ds(r, 1)], xbuf_hbm.at[pl.ds(dst, 1)], sem)

    def start(r, carry):
        copy(r, 0).start()
        copy(r, 1).start()
        return carry
    lax.fori_loop(0, tr, start, 0)

    def wait(r, carry):
        copy(r, 0).wait()
        copy(r, 1).wait()
        return carry
    lax.fori_loop(0, tr, wait, 0)


def _dispatch(h, e_flat, r_flat, pstarts, p_rows):
    t, w = h.shape
    tr = _tile(t, 256, SUBLANES)
    grid_spec = pltpu.PrefetchScalarGridSpec(
        num_scalar_prefetch=3, grid=(t // tr,),
        in_specs=[pl.BlockSpec((tr, w), lambda i, *_: (i, 0)), pl.BlockSpec(memory_space=pl.ANY)],
        out_specs=pl.BlockSpec(memory_space=pl.ANY),
        scratch_shapes=[pltpu.SemaphoreType.DMA(())])
    return pl.pallas_call(
        _dispatch_kernel, grid_spec=grid_spec,
        out_shape=jax.ShapeDtypeStruct((p_rows, w), h.dtype),
        input_output_aliases={4: 0}, compiler_params=_cparams(1),
    )(e_flat, r_flat, pstarts, h, jnp.zeros((p_rows, w), h.dtype))


def _combine_kernel(e_ref, r_ref, ps_ref, x_ref, w_ref, mod_ref, gain_ref, y_hbm, o_ref, ybuf, sem, *, res_row):
    tr = x_ref.shape[0]
    base = pl.program_id(0) * tr

    def copy(r, k):
        a = (base + r) * 2 + k
        src = ps_ref[e_ref[a]] + r_ref[a]
        return pltpu.make_async_copy(y_hbm.at[pl.ds(src, 1)], ybuf.at[k, pl.ds(r, 1)], sem)

    def start(r, carry):
        copy(r, 0).start()
        copy(r, 1).start()
        return carry
    lax.fori_loop(0, tr, start, 0)

    def wait(r, carry):
        copy(r, 0).wait()
        copy(r, 1).wait()
        return carry
    lax.fori_loop(0, tr, wait, 0)

    w = w_ref[...]
    y = ybuf[0] * w[:, 0:1] + ybuf[1] * w[:, 1:2]
    x = x_ref[...] + mod_ref[res_row:res_row + 1, :] * y
    o_ref[...] = _rms(x, gain_ref[...])


def _combine_final(x, ybuf, wts, mod, gain, e_flat, r_flat, pstarts, seq, res_row):
    t, d = x.shape
    tr = _tile(seq, 256, SUBLANES)
    per_b = seq // tr
    grid_spec = pltpu.PrefetchScalarGridSpec(
        num_scalar_prefetch=3, grid=(t // tr,),
        in_specs=[pl.BlockSpec((tr, d), lambda i, *_: (i, 0)), pl.BlockSpec((tr, LANES), lambda i, *_: (i, 0)),
                  pl.BlockSpec((None, SUBLANES, d), lambda i, *_: (i // per_b, 0, 0)),
                  pl.BlockSpec((1, d), lambda i, *_: (0, 0)), pl.BlockSpec(memory_space=pl.ANY)],
        out_specs=pl.BlockSpec((tr, d), lambda i, *_: (i, 0)),
        scratch_shapes=[pltpu.VMEM((2, tr, d), F32), pltpu.SemaphoreType.DMA(())])
    return pl.pallas_call(
        functools.partial(_combine_kernel, res_row=res_row), grid_spec=grid_spec,
        out_shape=jax.ShapeDtypeStruct((t, d), F32), compiler_params=_cparams(1),
    )(e_flat, r_flat, pstarts, x, wts, mod, gain.reshape(1, d), ybuf)


def _split_w_in(w_in, d):
    sizes = (DIL_WIDTH, DIL_WIDTH, DIL_WIDTH, MLA_Q_LORA, MLA_KV_LORA, MLA_ROPE,
             MOBA_HEADS * HEAD_DIM, MOBA_HEADS * HEAD_DIM, MOBA_HEADS * HEAD_DIM,
             DSA_HEADS * HEAD_DIM, HEAD_DIM, HEAD_DIM, IDX_HEADS * IDX_DIM, IDX_DIM, IDX_HEADS, N_BRANCH * d)
    names = ("a_q", "a_k", "a_v", "b_cq", "b_ckv", "b_kr", "c_q", "c_k", "c_v",
             "d_q", "d_k", "d_v", "d_iq", "d_ik", "d_iw", "g")
    assert w_in.shape[1] == sum(sizes)
    parts, o = {}, 0
    for nme, sz in zip(names, sizes):
        parts[nme] = w_in[:, o:o + sz]
        o += sz
    return parts


def _cat_cols(parts, names, pad_to=None):
    cols, offs, o = [], {}, 0
    for nme in names:
        wpart = parts[nme]
        width = -(-wpart.shape[1] // LANES) * LANES
        if width != wpart.shape[1]:
            wpart = jnp.pad(wpart, ((0, 0), (0, width - wpart.shape[1])))
        cols.append(wpart.astype(BF16))
        offs[nme] = o // LANES
        o += width
    if pad_to is not None and o % pad_to:
        cols.append(jnp.zeros((cols[0].shape[0], pad_to - o % pad_to), BF16))
    return jnp.concatenate(cols, axis=1), offs


def _mixer(h, x, mod, bsz, seq, tables, w_in, q_norm, q_up, kv_norm, kv_up, w_branch, w_out):
    t, d = h.shape
    rope_h, rope_iq, rope_ik, rope_mla = tables
    parts = _split_w_in(w_in, d)
    w_rot, ro = _cat_cols(parts, ("c_q", "c_k", "d_q"))
    w_plain, po = _cat_cols(parts, ("b_cq", "c_v", "b_ckv", "d_v", "b_kr"), pad_to=512)
    assert ro["d_q"] * LANES % (DSA_HEADS * HEAD_DIM) == 0
    assert po["b_cq"] * LANES % MLA_Q_LORA == 0 and po["b_ckv"] * LANES % MLA_KV_LORA == 0

    rot = _matmul(h, w_rot, out_dtype=BF16, tn_pref=1024, epi="rope", rope=rope_h, half=ROT_DIM // 2)
    plain = _matmul(h, w_plain, out_dtype=BF16, tn_pref=512)
    d_k = _matmul(h, parts["d_k"].astype(BF16), out_dtype=BF16, tn_pref=LANES, epi="rope", rope=rope_h, half=ROT_DIM // 2)
    gw = DIL_OUT
    a_qk = [_matmul(h, jnp.concatenate([parts["a_q"][:, g * gw:(g + 1) * gw], parts["a_k"][:, g * gw:(g + 1) * gw]],
                                       axis=1).astype(BF16),
                    out_dtype=BF16, tn_pref=1024, epi="rope", rope=rope_h, half=ROT_DIM // 2)
            for g in range(len(DIL_GROUPS))]
    a_v = [_matmul(h, parts["a_v"][:, g * gw:(g + 1) * gw].astype(BF16), out_dtype=BF16, tn_pref=1024)
           for g in range(len(DIL_GROUPS))]
    iq = _matmul(h, parts["d_iq"].astype(BF16), out_dtype=BF16, tn_pref=1024, epi="rope", rope=rope_iq, half=IDX_ROT // 2)
    ik = _matmul(h, _cat_cols(parts, ("d_ik",))[0], out_dtype=F32, tn_pref=LANES, epi="rope", rope=rope_ik, half=IDX_ROT // 2)
    iw = _matmul(h, _cat_cols(parts, ("d_iw",))[0], out_dtype=F32, tn_pref=LANES)
    gates = _matmul(h, parts["g"].astype(BF16), out_dtype=BF16, tn_pref=1024, epi="sigmoid")

    o_a = _dilated_mixture(a_qk, a_v, bsz, seq)

    hq = MLA_NOPE + MLA_ROPE
    wq = q_up.reshape(MLA_Q_LORA, MLA_HEADS, hq)
    wq = jnp.pad(wq, ((0, 0), (0, 0), (0, MLA_QK_PAD - hq))).reshape(MLA_Q_LORA, MLA_HEADS * MLA_QK_PAD).astype(BF16)
    wkv = kv_up.reshape(MLA_KV_LORA, MLA_HEADS, MLA_NOPE + MLA_V)
    wkv = jnp.concatenate([wkv[:, :, :MLA_NOPE].reshape(MLA_KV_LORA, -1), wkv[:, :, MLA_NOPE:].reshape(MLA_KV_LORA, -1)],
                          axis=1).astype(BF16)
    q_cat, k_cat, v_b = _mla_project(plain, po["b_cq"], po["b_ckv"], po["b_kr"], q_norm, wq, kv_norm, wkv, rope_mla)
    o_b = _flash(q_cat, k_cat, v_b, bsz, seq, MLA_HEADS, MLA_QK_PAD, MLA_V, 0, 0, 0, hq ** -0.5)

    kmean = _moba_kmean(rot, bsz, seq, ro["c_k"])
    o_c = _flash(rot, rot, plain, bsz, seq, MOBA_HEADS, HEAD_DIM, HEAD_DIM, ro["c_q"], ro["c_k"], po["c_v"],
                 HEAD_DIM ** -0.5, kmean=kmean)

    o_d = _dsa_attention(rot, d_k, plain, iq, ik, iw, bsz, seq, ro["d_q"], po["d_v"])

    sizes = (DIL_OUT, MLA_HEADS * MLA_V, MOBA_HEADS * HEAD_DIM, DSA_HEADS * HEAD_DIM)
    projs, o = [], 0
    for sz in sizes:
        projs.append(w_branch[o:o + sz].astype(BF16))
        o += sz
    y = _branch_merge((o_a, o_b, o_c, o_d), projs, gates, d)
    return _matmul(y, w_out.astype(BF16), out_dtype=F32, tn_pref=512, epi="resid", xres=x, mod=mod, res_row=2, seq=seq)


def kernel(x, c, positions, w_ada, b_ada, ada_table_0, mix_norm_0, w_in_0, mla_q_norm_0, mla_q_up_0, mla_kv_norm_0, mla_kv_up_0, w_branch_0, w_out_0, ffn_norm_0, ffn_gate_0, ffn_up_0, ffn_down_0, ada_table_1, mix_norm_1, w_in_1, mla_q_norm_1, mla_q_up_1, mla_kv_norm_1, mla_kv_up_1, w_branch_1, w_out_1, ffn_norm_1, router_1, expert_gate_1, expert_up_1, expert_down_1, final_norm):
    bsz, seq, d = x.shape
    t = bsz * seq
    xf = x.reshape(t, d)
    mod0, mod1 = _ada_mod(c, w_ada, b_ada, ada_table_0, ada_table_1)

    pos_b = jnp.broadcast_to(positions.reshape(t, 1).astype(F32), (t, LANES))
    tables = (_rope_tables(pos_b, _rope_pattern(HEAD_DIM, ROT_DIM, LANES)),
              _rope_tables(pos_b, _rope_pattern(IDX_DIM, IDX_ROT, LANES)),
              _rope_tables(pos_b, _rope_pattern(IDX_DIM, IDX_ROT, IDX_DIM)),
              _rope_tables(pos_b, _rope_pattern(MLA_ROPE, MLA_ROPE, MLA_ROPE)))

    (h,) = _norm_mod(xf, mod0, mix_norm_0, seq, shift_row=0, scale_row=1)
    xf = _mixer(h, xf, mod0, bsz, seq, tables, w_in_0, mla_q_norm_0, mla_q_up_0, mla_kv_norm_0, mla_kv_up_0, w_branch_0, w_out_0)
    (h,) = _norm_mod(xf, mod0, ffn_norm_0, seq, shift_row=3, scale_row=4)
    y = _ffn_dense(h, ffn_gate_0.astype(BF16), ffn_up_0.astype(BF16), ffn_down_0.astype(BF16))

    xf, h = _norm_mod(xf, mod1, mix_norm_1, seq, shift_row=0, scale_row=1, y=y, res_mod=mod0, res_row=5)
    xf = _mixer(h, xf, mod1, bsz, seq, tables, w_in_1, mla_q_norm_1, mla_q_up_1, mla_kv_norm_1, mla_kv_up_1, w_branch_1, w_out_1)
    w_r = jnp.pad(router_1, ((0, 0), (0, LANES - N_EXPERTS))).astype(BF16)
    h, logits = _norm_mod(xf, mod1, ffn_norm_1, seq, shift_row=3, scale_row=4, w_router=w_r, h_dtype=U32)
    info, wts, cnt = _route(logits)
    e_flat = info[:, 0:2].reshape(2 * t)
    r_flat = info[:, 2:4].reshape(2 * t)
    counts = cnt[0, :N_EXPERTS].astype(I32)
    rows = min(MOE_ROWS, t)
    padded = (counts + rows - 1) // rows * rows
    pends = jnp.cumsum(padded).astype(I32)
    pstarts = pends - padded
    p_rows = (-(-2 * t // rows) + N_EXPERTS) * rows
    xbuf = _dispatch(h, e_flat, r_flat, pstarts, p_rows)
    ybuf = _ffn_experts(xbuf, pends, expert_gate_1.astype(BF16), expert_up_1.astype(BF16), expert_down_1.astype(BF16), rows)
    out = _combine_final(xf, ybuf, wts, mod1, final_norm, e_flat, r_flat, pstarts, seq, res_row=5)
    return out.reshape(bsz, seq, d)
```

```python
import functools

import numpy as np
import jax
import jax.numpy as jnp
from jax import lax
from jax.experimental import pallas as pl
from jax.experimental.pallas import tpu as pltpu

F32, BF16, I32, U32 = jnp.float32, jnp.bfloat16, jnp.int32, jnp.uint32

LANES = 128
SUBLANES = 8
VMEM_LIMIT_BYTES = 56 * 1024 * 1024

HEAD_DIM = 128
ROT_DIM = HEAD_DIM // 4
ROPE_THETA = 500000.0
NORM_EPS = 1e-6
MASK_VALUE = -1e30
DIL_GROUPS = ((128, 1), (512, 4), (2048, 16))
DIL_HEADS_PER_GROUP = 4
DIL_HEADS = len(DIL_GROUPS) * DIL_HEADS_PER_GROUP
DIL_WIDTH = DIL_HEADS * HEAD_DIM
DIL_OUT = DIL_HEADS_PER_GROUP * HEAD_DIM
MLA_HEADS = 8
MLA_Q_LORA = 1536
MLA_KV_LORA = 512
MLA_NOPE = 128
MLA_ROPE = 64
MLA_V = 128
MLA_QK_PAD = 256
MOBA_HEADS = 8
MOBA_BLOCK = 256
MOBA_TOPK = 3
DSA_HEADS = 8
DSA_TOPK = 256
IDX_HEADS = 32
IDX_DIM = 64
IDX_ROT = IDX_DIM // 4
N_BRANCH = 4
N_EXPERTS = 8
N_MOD = 6
INT_MIN = -2147483648

DIL_RUN = 4
DIAG_ROWS = 512
DSA_TQ = 512
DSA_TK = 512
COUNT_ROWS = 64
MOE_ROWS = 512


def _cparams(n_axes):
    return pltpu.CompilerParams(dimension_semantics=("arbitrary",) * n_axes,
                                vmem_limit_bytes=VMEM_LIMIT_BYTES)


def _tile(n, pref, mult=LANES):
    if n <= pref:
        return n
    t = (pref // mult) * mult
    while t > mult and n % t:
        t -= mult
    assert n % t == 0, (n, pref, mult)
    return t


def _nt_dot(a, b):
    return lax.dot_general(a, b, (((1,), (1,)), ((), ())), preferred_element_type=F32)


def _pack_bf16_pairs(x):
    n = x.shape[1]
    bits = lax.bitcast_convert_type(x.astype(BF16).astype(F32), U32)
    return (bits[:, n // 2:] & jnp.uint32(0xFFFF0000)) | (bits[:, :n // 2] >> 16)


def _unpack_bf16_pairs(w):
    lo = lax.bitcast_convert_type(w << 16, F32).astype(BF16)
    hi = lax.bitcast_convert_type(w & jnp.uint32(0xFFFF0000), F32).astype(BF16)
    return lo, hi


def _rope_pattern(group, rot, active):
    half = rot // 2
    freqs = ROPE_THETA ** (-jnp.arange(half, dtype=F32) * (2.0 / rot))
    lane = np.arange(LANES)
    p = lane % group
    on = (lane < active) & (p < rot)
    idx = np.where(on, p % half, 0)
    freq = jnp.where(jnp.asarray(on), freqs[idx], 0.0)
    first = (on & (p < half)).astype(np.float32)
    second = (on & (p >= half)).astype(np.float32)
    pat = jnp.zeros((SUBLANES, LANES), F32)
    return pat.at[0].set(freq).at[1].set(jnp.asarray(first)).at[2].set(jnp.asarray(second))


def _rope_table_kernel(pos_ref, pat_ref, c_ref, sa_ref, sb_ref):
    ang = pos_ref[...] * pat_ref[0:1, :]
    c_ref[...] = jnp.cos(ang)
    sin = jnp.sin(ang)
    sa_ref[...] = -sin * pat_ref[1:2, :]
    sb_ref[...] = sin * pat_ref[2:3, :]


def _rope_tables(pos_b, pat):
    t = pos_b.shape[0]
    ts = _tile(t, 1024)
    spec = pl.BlockSpec((ts, LANES), lambda i: (i, 0))
    sds = jax.ShapeDtypeStruct((t, LANES), F32)
    return pl.pallas_call(
        _rope_table_kernel, grid=(t // ts,),
        in_specs=[spec, pl.BlockSpec((SUBLANES, LANES), lambda i: (0, 0))],
        out_specs=[spec, spec, spec], out_shape=[sds, sds, sds],
        compiler_params=_cparams(1))(pos_b, pat)


def _apply_rope(x, c, sa, sb, half):
    return x * c + pltpu.roll(x, LANES - half, 1) * sa + pltpu.roll(x, half, 1) * sb


def _ada_kernel(c_ref, w_ref, b_ref, t0_ref, t1_ref, o0_ref, o1_ref):
    c = c_ref[...]
    a = (c * jax.nn.sigmoid(c)).astype(BF16)
    acc = jnp.dot(a, w_ref[...].astype(BF16), preferred_element_type=F32) + b_ref[...]
    o0_ref[...] = acc + t0_ref[...]
    o1_ref[...] = acc + t1_ref[...]


def _ada_mod(c, w_ada, b_ada, table0, table1):
    b, d = c.shape
    n = w_ada.shape[1]
    bp = -(-b // SUBLANES) * SUBLANES
    cp = jnp.zeros((bp, d), F32).at[:b].set(c)
    tn = _tile(n, 512)
    row = pl.BlockSpec((1, tn), lambda j: (0, j))
    out = pl.BlockSpec((bp, tn), lambda j: (0, j))
    sds = jax.ShapeDtypeStruct((bp, n), F32)
    m0, m1 = pl.pallas_call(
        _ada_kernel, grid=(n // tn,),
        in_specs=[pl.BlockSpec((bp, d), lambda j: (0, 0)), pl.BlockSpec((d, tn), lambda j: (0, j)), row, row, row],
        out_specs=[out, out], out_shape=[sds, sds], compiler_params=_cparams(1),
    )(cp, w_ada, b_ada.reshape(1, n), table0.reshape(1, n), table1.reshape(1, n))

    def pack(m):
        m = m[:b].reshape(b, N_MOD, d)
        return jnp.concatenate([m, jnp.zeros((b, SUBLANES - N_MOD, d), F32)], axis=1)
    return pack(m0), pack(m1)


def _norm_kernel(*refs, res_row, shift_row, scale_row, with_logits, h_dtype):
    it = iter(refs)
    x_ref = next(it)
    y_ref, rmod_ref = (next(it), next(it)) if res_row is not None else (None, None)
    mod_ref, gain_ref = next(it), next(it)
    wr_ref = next(it) if with_logits else None
    xo_ref = next(it) if res_row is not None else None
    h_ref = next(it)
    lg_ref = next(it) if with_logits else None

    x = x_ref[...]
    if res_row is not None:
        x = x + rmod_ref[res_row:res_row + 1, :] * y_ref[...].astype(F32)
        xo_ref[...] = x
    y = x * lax.rsqrt(jnp.mean(x * x, axis=-1, keepdims=True) + NORM_EPS) * gain_ref[...]
    h = y * (1.0 + mod_ref[scale_row:scale_row + 1, :]) + mod_ref[shift_row:shift_row + 1, :]
    if h_dtype == U32:
        h_ref[...] = _pack_bf16_pairs(h)
    else:
        h_ref[...] = h.astype(h_dtype)
    if with_logits:
        lg_ref[...] = jnp.dot(h.astype(BF16), wr_ref[...], preferred_element_type=F32)


def _norm_mod(x, mod, gain, seq, *, shift_row, scale_row, y=None, res_mod=None, res_row=None, w_router=None,
              h_dtype=BF16):
    t, d = x.shape
    ts = _tile(seq, 256, SUBLANES)
    per_b = seq // ts
    blk = pl.BlockSpec((ts, d), lambda i: (i, 0))
    mod_spec = pl.BlockSpec((None, SUBLANES, d), lambda i: (i // per_b, 0, 0))
    in_specs, args = [blk], [x]
    if res_row is not None:
        in_specs += [blk, mod_spec]
        args += [y, res_mod]
    in_specs += [mod_spec, pl.BlockSpec((1, d), lambda i: (0, 0))]
    args += [mod, gain.reshape(1, d)]
    if w_router is not None:
        in_specs.append(pl.BlockSpec((d, LANES), lambda i: (0, 0)))
        args.append(w_router)
    out_specs, out_shape = [], []
    if res_row is not None:
        out_specs.append(blk)
        out_shape.append(jax.ShapeDtypeStruct((t, d), F32))
    hd = d // 2 if h_dtype == U32 else d
    out_specs.append(pl.BlockSpec((ts, hd), lambda i: (i, 0)))
    out_shape.append(jax.ShapeDtypeStruct((t, hd), h_dtype))
    if w_router is not None:
        out_specs.append(pl.BlockSpec((ts, LANES), lambda i: (i, 0)))
        out_shape.append(jax.ShapeDtypeStruct((t, LANES), F32))
    kern = functools.partial(_norm_kernel, res_row=res_row, shift_row=shift_row, scale_row=scale_row,
                             with_logits=w_router is not None, h_dtype=h_dtype)
    return pl.pallas_call(kern, grid=(t // ts,), in_specs=in_specs, out_specs=out_specs, out_shape=out_shape,
                          compiler_params=_cparams(1))(*args)


def _mm_kernel(*refs, epi, half, res_row):
    a_ref, w_ref = refs[0], refs[1]
    o_ref = refs[-1]
    acc = jnp.dot(a_ref[...], w_ref[...], preferred_element_type=F32)
    if epi == "plain":
        o_ref[...] = acc.astype(o_ref.dtype)
    elif epi == "sigmoid":
        o_ref[...] = jax.nn.sigmoid(acc).astype(o_ref.dtype)
    elif epi == "rope":
        c, sa, sb = refs[2][...], refs[3][...], refs[4][...]
        for ch in range(acc.shape[1] // LANES):
            sl = slice(ch * LANES, (ch + 1) * LANES)
            o_ref[:, sl] = _apply_rope(acc[:, sl], c, sa, sb, half).astype(o_ref.dtype)
    elif epi == "resid":
        xres_ref, mod_ref = refs[2], refs[3]
        o_ref[...] = xres_ref[...] + mod_ref[res_row:res_row + 1, :] * acc
    else:
        raise ValueError(epi)


def _matmul(a, w, *, out_dtype, tn_pref, epi="plain", rope=None, half=0, xres=None, mod=None, res_row=None, seq=None):
    m, k = a.shape
    n = w.shape[1]
    tm = _tile(seq if seq is not None else m, 1024, SUBLANES)
    tn = _tile(n, tn_pref)
    in_specs = [pl.BlockSpec((tm, k), lambda i, j: (i, 0)), pl.BlockSpec((k, tn), lambda i, j: (0, j))]
    args = [a, w]
    if epi == "rope":
        in_specs += [pl.BlockSpec((tm, LANES), lambda i, j: (i, 0))] * 3
        args += list(rope)
    if epi == "resid":
        per_b = seq // tm
        in_specs += [pl.BlockSpec((tm, tn), lambda i, j: (i, j)),
                     pl.BlockSpec((None, SUBLANES, tn), lambda i, j: (i // per_b, 0, j))]
        args += [xres, mod]
    kern = functools.partial(_mm_kernel, epi=epi, half=half, res_row=res_row)
    return pl.pallas_call(
        kern, grid=(m // tm, n // tn), in_specs=in_specs,
        out_specs=pl.BlockSpec((tm, tn), lambda i, j: (i, j)),
        out_shape=jax.ShapeDtypeStruct((m, n), out_dtype), compiler_params=_cparams(2))(*args)


def _dil_kernel(q_ref, kp_ref, ko_ref, vp_ref, vo_ref, o_ref, lse_ref, *, scale):
    i = pl.program_id(2)
    w = kp_ref.shape[0]
    row = lax.broadcasted_iota(I32, (w, w), 0)
    col = lax.broadcasted_iota(I32, (w, w), 1)
    for j in range(q_ref.shape[0] // w):
        rs = slice(j * w, (j + 1) * w)
        ps = slice((j - 1) * w, j * w)
        for h in range(DIL_HEADS_PER_GROUP):
            sl = slice(h * HEAD_DIM, (h + 1) * HEAD_DIM)
            q = q_ref[rs, sl]
            k_prev = kp_ref[:, sl] if j == 0 else ko_ref[ps, sl]
            v_prev = vp_ref[:, sl] if j == 0 else vo_ref[ps, sl]
            s_o = jnp.where(col <= row, _nt_dot(q, ko_ref[rs, sl]) * scale, MASK_VALUE)
            s_p = jnp.where(col >= row, _nt_dot(q, k_prev) * scale, MASK_VALUE)
            if j == 0:
                s_p = jnp.where(i > 0, s_p, MASK_VALUE)
            m = jnp.maximum(jnp.max(s_o, axis=1, keepdims=True), jnp.max(s_p, axis=1, keepdims=True))
            p_o = jnp.exp(s_o - m)
            p_p = jnp.exp(s_p - m)
            l = jnp.sum(p_o, axis=1, keepdims=True) + jnp.sum(p_p, axis=1, keepdims=True)
            o = (jnp.dot(p_o.astype(BF16), vo_ref[rs, sl], preferred_element_type=F32)
                 + jnp.dot(p_p.astype(BF16), v_prev, preferred_element_type=F32))
            o_ref[rs, sl] = o / l
            lse_ref[rs, sl] = jnp.broadcast_to(m + jnp.log(l), o.shape)


def _dilated_group(qk, v, bsz, seq, g):
    window, dil = DIL_GROUPS[g]
    wsub = window // dil
    n = seq // dil
    assert wsub == HEAD_DIM and n % wsub == 0
    nb = n // wsub
    qk_v = qk.reshape(bsz, n, dil * 2 * DIL_OUT)
    v_v = v.reshape(bsz, n, dil * DIL_OUT)
    run = min(DIL_RUN, nb)
    assert nb % run == 0
    blk, pblk = (None, run * wsub, DIL_OUT), (None, wsub, DIL_OUT)
    q_spec = pl.BlockSpec(blk, lambda b, r, i: (b, i, 2 * r))
    k_own = pl.BlockSpec(blk, lambda b, r, i: (b, i, 2 * r + 1))
    k_prev = pl.BlockSpec(pblk, lambda b, r, i: (b, jnp.maximum(i * run - 1, 0), 2 * r + 1))
    v_own = pl.BlockSpec(blk, lambda b, r, i: (b, i, r))
    v_prev = pl.BlockSpec(pblk, lambda b, r, i: (b, jnp.maximum(i * run - 1, 0), r))
    sds = jax.ShapeDtypeStruct((bsz, n, dil * DIL_OUT), F32)
    o, lse = pl.pallas_call(
        functools.partial(_dil_kernel, scale=HEAD_DIM ** -0.5), grid=(bsz, dil, nb // run),
        in_specs=[q_spec, k_prev, k_own, v_prev, v_own],
        out_specs=[v_own, v_own], out_shape=[sds, sds], compiler_params=_cparams(3),
    )(qk_v, qk_v, qk_v, v_v, v_v)
    return o.reshape(bsz * seq, DIL_OUT), lse.reshape(bsz * seq, DIL_OUT)


def _dil_combine_kernel(o0, o1, o2, l0, l1, l2, out_ref):
    a, b, c = l0[...], l1[...], l2[...]
    m = jnp.maximum(jnp.maximum(a, b), c)
    ea, eb, ec = jnp.exp(a - m), jnp.exp(b - m), jnp.exp(c - m)
    out_ref[...] = ((ea * o0[...] + eb * o1[...] + ec * o2[...]) / (ea + eb + ec)).astype(out_ref.dtype)


def _dilated_mixture(qks, vs, bsz, seq):
    outs, lses = zip(*[_dilated_group(qks[g], vs[g], bsz, seq, g) for g in range(len(DIL_GROUPS))])
    t = bsz * seq
    ts = _tile(t, 1024, SUBLANES)
    spec = pl.BlockSpec((ts, DIL_OUT), lambda i: (i, 0))
    return pl.pallas_call(
        _dil_combine_kernel, grid=(t // ts,), in_specs=[spec] * 6, out_specs=spec,
        out_shape=jax.ShapeDtypeStruct((t, DIL_OUT), BF16), compiler_params=_cparams(1))(*outs, *lses)


def _rms(x, gain):
    return x * lax.rsqrt(jnp.mean(x * x, axis=-1, keepdims=True) + NORM_EPS) * gain


def _mla_q_kernel(cq_ref, gain_ref, w_ref, c_ref, sa_ref, sb_ref, o_ref):
    cqn = _rms(cq_ref[...].astype(F32), gain_ref[...]).astype(BF16)
    acc = jnp.dot(cqn, w_ref[...], preferred_element_type=F32)
    c, sa, sb = c_ref[...], sa_ref[...], sb_ref[...]
    for ch in range(acc.shape[1] // LANES):
        sl = slice(ch * LANES, (ch + 1) * LANES)
        x = acc[:, sl]
        if ch % 2 == 1:
            x = _apply_rope(x, c, sa, sb, MLA_ROPE // 2)
        o_ref[:, sl] = x.astype(o_ref.dtype)


def _mla_kv_kernel(ckv_ref, gain_ref, w_ref, kr_ref, c_ref, sa_ref, sb_ref, k_ref, v_ref):
    ckvn = _rms(ckv_ref[...].astype(F32), gain_ref[...]).astype(BF16)
    acc = jnp.dot(ckvn, w_ref[...], preferred_element_type=F32)
    kpe = _apply_rope(kr_ref[...].astype(F32), c_ref[...], sa_ref[...], sb_ref[...], MLA_ROPE // 2).astype(k_ref.dtype)
    hw = MLA_HEADS * MLA_NOPE
    for h in range(MLA_HEADS):
        k_ref[:, h * MLA_QK_PAD:h * MLA_QK_PAD + MLA_NOPE] = acc[:, h * MLA_NOPE:(h + 1) * MLA_NOPE].astype(k_ref.dtype)
        k_ref[:, h * MLA_QK_PAD + MLA_NOPE:(h + 1) * MLA_QK_PAD] = kpe
    v_ref[...] = acc[:, hw:].astype(v_ref.dtype)


def _mla_project(plain, cq_col, ckv_col, kr_col, q_gain, wq, kv_gain, wkv, rope):
    t = plain.shape[0]
    ts = _tile(t, 512, SUBLANES)
    tab = pl.BlockSpec((ts, LANES), lambda i: (i, 0))
    nq = MLA_HEADS * MLA_QK_PAD
    q_cat = pl.pallas_call(
        _mla_q_kernel, grid=(t // ts,),
        in_specs=[pl.BlockSpec((ts, MLA_Q_LORA), lambda i: (i, cq_col * LANES // MLA_Q_LORA)),
                  pl.BlockSpec((1, MLA_Q_LORA), lambda i: (0, 0)),
                  pl.BlockSpec((MLA_Q_LORA, nq), lambda i: (0, 0)), tab, tab, tab],
        out_specs=pl.BlockSpec((ts, nq), lambda i: (i, 0)),
        out_shape=jax.ShapeDtypeStruct((t, nq), BF16), compiler_params=_cparams(1),
    )(plain, q_gain.reshape(1, -1), wq, *rope)
    nv = MLA_HEADS * MLA_V
    k_cat, v = pl.pallas_call(
        _mla_kv_kernel, grid=(t // ts,),
        in_specs=[pl.BlockSpec((ts, MLA_KV_LORA), lambda i: (i, ckv_col * LANES // MLA_KV_LORA)),
                  pl.BlockSpec((1, MLA_KV_LORA), lambda i: (0, 0)),
                  pl.BlockSpec((MLA_KV_LORA, MLA_HEADS * MLA_NOPE + nv), lambda i: (0, 0)),
                  pl.BlockSpec((ts, LANES), lambda i: (i, kr_col)), tab, tab, tab],
        out_specs=[pl.BlockSpec((ts, nq), lambda i: (i, 0)), pl.BlockSpec((ts, nv), lambda i: (i, 0))],
        out_shape=[jax.ShapeDtypeStruct((t, nq), BF16), jax.ShapeDtypeStruct((t, nv), BF16)],
        compiler_params=_cparams(1),
    )(plain, kv_gain.reshape(1, -1), wkv, plain, *rope)
    return q_cat, k_cat, v


def _softmax_step(s, v, m_sc, l_sc, acc_sc, rows):
    tiles = [s[:, j * LANES:(j + 1) * LANES] for j in range(s.shape[1] // LANES)]
    m_prev = m_sc[rows, :]
    m_new = jnp.maximum(m_prev, jnp.max(functools.reduce(jnp.maximum, tiles), axis=1, keepdims=True))
    alpha = jnp.exp(m_prev - m_new)
    ps = [jnp.exp(x - m_new) for x in tiles]
    l_sc[rows, :] = alpha * l_sc[rows, :] + jnp.sum(functools.reduce(jnp.add, ps), axis=1, keepdims=True)
    p = jnp.concatenate([x.astype(BF16) for x in ps], axis=1)
    acc_sc[rows, :] = alpha * acc_sc[rows, :] + jnp.dot(p, v, preferred_element_type=F32)
    m_sc[rows, :] = m_new


def _flash_kernel(*refs, scale, moba):
    if moba:
        qi_ref, st_ref, q_ref, k_ref, v_ref, km_ref, o_ref, m_sc, l_sc, acc_sc, sel_sc = refs
    else:
        qi_ref, st_ref, q_ref, k_ref, v_ref, o_ref, m_sc, l_sc, acc_sc = refs
    qi, step = qi_ref[pl.program_id(2)], st_ref[pl.program_id(2)]
    kv = qi - step
    tq, tk = q_ref.shape[0], k_ref.shape[0]

    @pl.when(step == 0)
    def _init():
        m_sc[...] = jnp.full(m_sc.shape, MASK_VALUE, F32)
        l_sc[...] = jnp.zeros(l_sc.shape, F32)
        acc_sc[...] = jnp.zeros(acc_sc.shape, F32)
        if moba:
            lane = lax.broadcasted_iota(I32, (tq, LANES), 1)
            own = (qi * tq + lax.broadcasted_iota(I32, (tq, LANES), 0)) // MOBA_BLOCK
            valid = lane < own
            g = jnp.where(valid, _nt_dot(q_ref[...], km_ref[...]), MASK_VALUE)
            sel = jnp.zeros((tq, LANES), F32)
            lane_f = lane.astype(F32)
            for _ in range(MOBA_TOPK):
                mx = jnp.max(g, axis=1, keepdims=True)
                first = jnp.min(jnp.where(g == mx, lane_f, float(LANES)), axis=1, keepdims=True)
                pick = lane_f == first
                sel = jnp.where(pick, 1.0, sel)
                g = jnp.where(pick, -jnp.inf, g)
            sel_sc[...] = jnp.where(valid, sel, 0.0)

    def block_flags(rows, n_rows, n_keys):
        lane = lax.broadcasted_iota(I32, (n_rows, LANES), 1)
        sel = sel_sc[rows, :]
        nsub = tk // MOBA_BLOCK
        flags = []
        for c in range(n_keys // MOBA_BLOCK):
            f = jnp.sum(jnp.where(lane == kv * nsub + c, sel, 0.0), axis=1, keepdims=True)
            flags.append(jnp.broadcast_to(f, (n_rows, MOBA_BLOCK)))
        return flags[0] if len(flags) == 1 else jnp.concatenate(flags, axis=1)

    @pl.when(step == 0)
    def _diagonal():
        dc = min(DIAG_ROWS, tq)
        for r0 in range(0, tq, dc):
            rows, kw = slice(r0, r0 + dc), r0 + dc
            s = _nt_dot(q_ref[rows, :], k_ref[0:kw, :]) * scale
            row = r0 + lax.broadcasted_iota(I32, (dc, kw), 0)
            col = lax.broadcasted_iota(I32, (dc, kw), 1)
            ok = col <= row
            if moba:
                same = (col // MOBA_BLOCK) == (row // MOBA_BLOCK)
                ok = jnp.logical_and(ok, jnp.logical_or(same, block_flags(rows, dc, kw) > 0.0))
            _softmax_step(jnp.where(ok, s, MASK_VALUE), v_ref[0:kw, :], m_sc, l_sc, acc_sc, rows)

    @pl.when(step > 0)
    def _past():
        s = _nt_dot(q_ref[...], k_ref[...]) * scale
        if moba:
            s = jnp.where(block_flags(slice(None), tq, tk) > 0.0, s, MASK_VALUE)
        _softmax_step(s, v_ref[...], m_sc, l_sc, acc_sc, slice(None))

    @pl.when(step == qi)
    def _fin():
        o_ref[...] = (acc_sc[...] / l_sc[:, 0:1]).astype(o_ref.dtype)


def _flash(q, k, v, bsz, seq, heads, dq, dv, q_col, k_col, v_col, scale, kmean=None):
    t = bsz * seq
    tq = _tile(seq, 1024, MOBA_BLOCK)
    nq = seq // tq
    pairs = [(i, s) for i in range(nq) for s in range(i + 1)]
    qi_arr = jnp.asarray([p[0] for p in pairs], I32)
    st_arr = jnp.asarray([p[1] for p in pairs], I32)
    in_specs = [pl.BlockSpec((tq, dq), lambda b, h, p, qi, st: (b * nq + qi[p], q_col + h)),
                pl.BlockSpec((tq, dq), lambda b, h, p, qi, st: (b * nq + qi[p] - st[p], k_col + h)),
                pl.BlockSpec((tq, dv), lambda b, h, p, qi, st: (b * nq + qi[p] - st[p], v_col + h))]
    args = [q, k, v]
    scratch = [pltpu.VMEM((tq, LANES), F32), pltpu.VMEM((tq, LANES), F32), pltpu.VMEM((tq, dv), F32)]
    if kmean is not None:
        in_specs.append(pl.BlockSpec((None, None, LANES, HEAD_DIM), lambda b, h, p, qi, st: (b, h, 0, 0)))
        args.append(kmean)
        scratch.append(pltpu.VMEM((tq, LANES), F32))
    grid_spec = pltpu.PrefetchScalarGridSpec(
        num_scalar_prefetch=2, grid=(bsz, heads, len(pairs)), in_specs=in_specs,
        out_specs=pl.BlockSpec((tq, dv), lambda b, h, p, qi, st: (b * nq + qi[p], h)),
        scratch_shapes=scratch)
    return pl.pallas_call(
        functools.partial(_flash_kernel, scale=scale, moba=kmean is not None), grid_spec=grid_spec,
        out_shape=jax.ShapeDtypeStruct((t, heads * dv), BF16), compiler_params=_cparams(3))(qi_arr, st_arr, *args)


def _kmean_kernel(k_ref, o_ref):
    s = k_ref.shape[0]
    nb = s // MOBA_BLOCK
    k = k_ref[...].astype(F32).reshape(nb, MOBA_BLOCK, HEAD_DIM)
    mean = jnp.sum(k, axis=1) * (1.0 / MOBA_BLOCK)
    full = jnp.concatenate([mean, jnp.zeros((LANES - nb, HEAD_DIM), F32)], axis=0)
    o_ref[...] = full.astype(o_ref.dtype)


def _moba_kmean(rot, bsz, seq, k_col):
    assert seq % MOBA_BLOCK == 0 and seq // MOBA_BLOCK <= LANES
    return pl.pallas_call(
        _kmean_kernel, grid=(bsz, MOBA_HEADS),
        in_specs=[pl.BlockSpec((seq, HEAD_DIM), lambda b, h: (b, k_col + h))],
        out_specs=pl.BlockSpec((None, None, LANES, HEAD_DIM), lambda b, h: (b, h, 0, 0)),
        out_shape=jax.ShapeDtypeStruct((bsz, MOBA_HEADS, LANES, HEAD_DIM), BF16),
        compiler_params=_cparams(2))(rot)


def _dsa_kernel(q_ref, iq_ref, iw_ref, ikbd_ref, k_ref, v_ref, o_ref, key_sc, cand_sc, cnt_sc, qs_sc, m_sc, l_sc, acc_sc, *, n_keep, scale,
                wscale):
    c = pl.program_id(1)
    tq, tk = DSA_TQ, DSA_TK
    nkt = (c * tq) // tk + 1
    rowg = c * tq + lax.broadcasted_iota(I32, (tq, tk), 0)
    col = lax.broadcasted_iota(I32, (tq, tk), 1)
    w = iw_ref[...] * wscale

    def score_tile(kt, carry):
        ikb = ikbd_ref[kt]
        sc = jnp.zeros((tq, tk), F32)
        for j in range(IDX_HEADS // 2):
            rel = jnp.maximum(_nt_dot(iq_ref[:, j * LANES:(j + 1) * LANES], ikb), 0.0)
            sc = sc + rel[:, :tk] * w[:, 2 * j:2 * j + 1] + rel[:, tk:] * w[:, 2 * j + 1:2 * j + 2]
        bits = lax.bitcast_convert_type(sc, I32)
        key = bits ^ ((bits >> 31) & 0x7FFFFFFF)
        key_sc[kt] = jnp.where(kt * tk + col <= rowg, key, INT_MIN)
        return carry
    lax.fori_loop(0, nkt, score_tile, 0)

    def count_ge(cand):
        cand_sc[...] = cand
        cnt_sc[...] = jnp.zeros(cnt_sc.shape, F32)

        def body(kt, carry):
            for r0 in range(0, tq, COUNT_ROWS):
                rows = slice(r0, r0 + COUNT_ROWS)
                c = cand_sc[rows, :]
                a = cnt_sc[rows, :]
                for j in range(tk // LANES):
                    a = a + jnp.where(key_sc[kt, rows, j * LANES:(j + 1) * LANES] >= c, 1.0, 0.0)
                cnt_sc[rows, :] = a
            return carry
        lax.fori_loop(0, nkt, body, 0)
        return jnp.broadcast_to(jnp.sum(cnt_sc[...], axis=1, keepdims=True), (tq, LANES))

    keep = jnp.float32(n_keep)
    thr = jnp.where(count_ge(jnp.zeros((tq, LANES), I32)) >= keep, 0, INT_MIN).astype(I32)

    def search(it, thr):
        cand = thr | (jnp.int32(1) << (30 - it))
        return jnp.where(count_ge(cand) >= keep, cand, thr)
    thr = lax.fori_loop(0, 31, search, thr)
    thr = jnp.maximum(thr, INT_MIN + 1)

    for h in range(DSA_HEADS):
        qs_sc[h * tq:(h + 1) * tq, :] = q_ref[:, h * HEAD_DIM:(h + 1) * HEAD_DIM]
    m_sc[...] = jnp.full(m_sc.shape, MASK_VALUE, F32)
    l_sc[...] = jnp.zeros(l_sc.shape, F32)
    acc_sc[...] = jnp.zeros(acc_sc.shape, F32)

    def attend(kt, carry):
        sel = jnp.concatenate([key_sc[kt, :, j * LANES:(j + 1) * LANES] >= thr for j in range(tk // LANES)], axis=1)
        ks = k_ref[pl.ds(pl.multiple_of(kt * tk, tk), tk), :]
        vs = v_ref[pl.ds(pl.multiple_of(kt * tk, tk), tk), :]
        s = _nt_dot(qs_sc[...], ks) * scale
        s = jnp.concatenate([jnp.where(sel, s[h * tq:(h + 1) * tq, :], MASK_VALUE) for h in range(DSA_HEADS)], axis=0)
        _softmax_step(s, vs, m_sc, l_sc, acc_sc, slice(None))
        return carry
    lax.fori_loop(0, nkt, attend, 0)

    for h in range(DSA_HEADS):
        rows = slice(h * tq, (h + 1) * tq)
        o_ref[:, h * HEAD_DIM:(h + 1) * HEAD_DIM] = (acc_sc[rows, :] / l_sc[rows, 0:1]).astype(o_ref.dtype)


def _dsa_attention(rot, k, plain, iq, ik, iw, bsz, seq, q_col, v_col):
    t = bsz * seq
    assert seq % DSA_TK == 0
    nkt = seq // DSA_TK
    nch = seq // DSA_TQ
    ika = ik.astype(BF16).reshape(bsz, nkt, DSA_TK, LANES)
    ikbd = jnp.concatenate([ika, jnp.roll(ika, IDX_DIM, axis=-1)], axis=2)
    n_keep = min(DSA_TOPK, seq // 4)
    qw = DSA_HEADS * HEAD_DIM
    iqw = IDX_HEADS * IDX_DIM
    kern = functools.partial(_dsa_kernel, n_keep=n_keep, scale=HEAD_DIM ** -0.5,
                             wscale=(IDX_HEADS ** -0.5) * (IDX_DIM ** -0.5))
    return pl.pallas_call(
        kern, grid=(bsz, nch),
        in_specs=[pl.BlockSpec((DSA_TQ, qw), lambda b, c: (b * nch + c, q_col * HEAD_DIM // qw)),
                  pl.BlockSpec((DSA_TQ, iqw), lambda b, c: (b * nch + c, 0)),
                  pl.BlockSpec((DSA_TQ, LANES), lambda b, c: (b * nch + c, 0)),
                  pl.BlockSpec((None, nkt, 2 * DSA_TK, LANES), lambda b, c: (b, 0, 0, 0)),
                  pl.BlockSpec((seq, HEAD_DIM), lambda b, c: (b, 0)),
                  pl.BlockSpec((seq, HEAD_DIM), lambda b, c: (b, v_col))],
        out_specs=pl.BlockSpec((DSA_TQ, qw), lambda b, c: (b * nch + c, 0)),
        out_shape=jax.ShapeDtypeStruct((t, qw), BF16),
        scratch_shapes=[pltpu.VMEM((nkt, DSA_TQ, DSA_TK), I32), pltpu.VMEM((DSA_TQ, LANES), I32),
                        pltpu.VMEM((DSA_TQ, LANES), F32), pltpu.VMEM((DSA_HEADS * DSA_TQ, HEAD_DIM), BF16),
                        pltpu.VMEM((DSA_HEADS * DSA_TQ, LANES), F32), pltpu.VMEM((DSA_HEADS * DSA_TQ, LANES), F32),
                        pltpu.VMEM((DSA_HEADS * DSA_TQ, HEAD_DIM), F32)],
        compiler_params=_cparams(2))(rot, iq, iw, ikbd, k, plain)


def _branch_kernel(oa, ob, oc, od, pa, pb, pc, pd, ga, gb, gc, gd, y_ref):
    y = None
    for o, p, g in ((oa, pa, ga), (ob, pb, gb), (oc, pc, gc), (od, pd, gd)):
        term = g[...].astype(F32) * jnp.dot(o[...], p[...], preferred_element_type=F32)
        y = term if y is None else y + term
    y_ref[...] = y.astype(y_ref.dtype)


def _branch_merge(outs, projs, gates, d):
    t = outs[0].shape[0]
    tm = _tile(t, 1024, SUBLANES)
    tn = _tile(d, 512)
    nj = d // tn
    in_specs = [pl.BlockSpec((tm, o.shape[1]), lambda i, j: (i, 0)) for o in outs]
    in_specs += [pl.BlockSpec((p.shape[0], tn), lambda i, j: (0, j)) for p in projs]
    in_specs += [pl.BlockSpec((tm, tn), functools.partial(lambda i, j, k: (i, k * nj + j), k=k)) for k in range(N_BRANCH)]
    return pl.pallas_call(
        _branch_kernel, grid=(t // tm, nj), in_specs=in_specs,
        out_specs=pl.BlockSpec((tm, tn), lambda i, j: (i, j)),
        out_shape=jax.ShapeDtypeStruct((t, d), BF16), compiler_params=_cparams(2),
    )(*outs, *projs, gates, gates, gates, gates)


def _glu_kernel(*refs, n_prefetch):
    pends_ref = refs[0] if n_prefetch else None
    x_ref, wg_ref, wu_ref, o_ref = refs[n_prefetch:n_prefetch + 4]

    def compute():
        if x_ref.dtype == U32:
            xb_sc = refs[n_prefetch + 4]

            @pl.when(pl.program_id(1) == 0)
            def _():
                lo, hi = _unpack_bf16_pairs(x_ref[...])
                xb_sc[:, :lo.shape[1]] = lo
                xb_sc[:, lo.shape[1]:] = hi
            x = xb_sc[...]
        else:
            x = x_ref[...]
        g = jnp.dot(x, wg_ref[...], preferred_element_type=F32)
        u = jnp.dot(x, wu_ref[...], preferred_element_type=F32)
        o_ref[...] = (g * jax.nn.sigmoid(g) * u).astype(o_ref.dtype)

    if n_prefetch:
        used = pl.program_id(0) * x_ref.shape[0] < pends_ref[N_EXPERTS - 1]
        pl.when(used)(compute)

        @pl.when(jnp.logical_not(used))
        def _():
            o_ref[...] = jnp.zeros(o_ref.shape, o_ref.dtype)
    else:
        compute()


def _mmk_kernel(a_ref, w_ref, o_ref, acc_sc):
    k = pl.program_id(2)
    part = jnp.dot(a_ref[...], w_ref[...], preferred_element_type=F32)

    @pl.when(k == 0)
    def _():
        acc_sc[...] = part

    @pl.when(k > 0)
    def _():
        acc_sc[...] += part

    @pl.when(k == pl.num_programs(2) - 1)
    def _():
        o_ref[...] = acc_sc[...].astype(o_ref.dtype)


def _ffn_dense(h, wg, wu, wd):
    t, d = h.shape
    ff = wg.shape[1]
    tm = _tile(t, 1024, SUBLANES)
    tf = _tile(ff, 512)
    a = pl.pallas_call(
        functools.partial(_glu_kernel, n_prefetch=0), grid=(t // tm, ff // tf),
        in_specs=[pl.BlockSpec((tm, d), lambda i, f: (i, 0)), pl.BlockSpec((d, tf), lambda i, f: (0, f)),
                  pl.BlockSpec((d, tf), lambda i, f: (0, f))],
        out_specs=pl.BlockSpec((tm, tf), lambda i, f: (i, f)),
        out_shape=jax.ShapeDtypeStruct((t, ff), BF16), compiler_params=_cparams(2))(h, wg, wu)
    tn = _tile(d, 2048)
    tk = _tile(ff, 2048)
    return pl.pallas_call(
        _mmk_kernel, grid=(t // tm, d // tn, ff // tk),
        in_specs=[pl.BlockSpec((tm, tk), lambda i, j, k: (i, k)), pl.BlockSpec((tk, tn), lambda i, j, k: (k, j))],
        out_specs=pl.BlockSpec((tm, tn), lambda i, j, k: (i, j)),
        out_shape=jax.ShapeDtypeStruct((t, d), BF16),
        scratch_shapes=[pltpu.VMEM((tm, tn), F32)], compiler_params=_cparams(3))(a, wd)


def _down_experts_kernel(pends_ref, a_ref, w_ref, o_ref):
    used = pl.program_id(0) * a_ref.shape[0] < pends_ref[N_EXPERTS - 1]

    @pl.when(used)
    def _():
        o_ref[...] = jnp.dot(a_ref[...], w_ref[...], preferred_element_type=F32)

    @pl.when(jnp.logical_not(used))
    def _():
        o_ref[...] = jnp.zeros(o_ref.shape, o_ref.dtype)


def _ffn_experts(xbuf, pends, wg, wu, wd, rows):
    p = xbuf.shape[0]
    d, ff = wg.shape[1], wg.shape[2]
    tf = _tile(ff, 512)

    def expert(i, pends_ref):
        e = jnp.int32(0)
        for j in range(N_EXPERTS - 1):
            e = e + (i * rows >= pends_ref[j]).astype(I32)
        return e

    a = pl.pallas_call(
        functools.partial(_glu_kernel, n_prefetch=1),
        grid_spec=pltpu.PrefetchScalarGridSpec(
            num_scalar_prefetch=1, grid=(p // rows, ff // tf),
            in_specs=[pl.BlockSpec((rows, d // 2), lambda i, f, pe: (i, 0)),
                      pl.BlockSpec((None, d, tf), lambda i, f, pe: (expert(i, pe), 0, f)),
                      pl.BlockSpec((None, d, tf), lambda i, f, pe: (expert(i, pe), 0, f))],
            out_specs=pl.BlockSpec((rows, tf), lambda i, f, pe: (i, f)),
            scratch_shapes=[pltpu.VMEM((rows, d), BF16)]),
        out_shape=jax.ShapeDtypeStruct((p, ff), BF16), compiler_params=_cparams(2))(pends, xbuf, wg, wu)
    tn = _tile(d, 1024)
    return pl.pallas_call(
        _down_experts_kernel,
        grid_spec=pltpu.PrefetchScalarGridSpec(
            num_scalar_prefetch=1, grid=(p // rows, d // tn),
            in_specs=[pl.BlockSpec((rows, ff), lambda i, j, pe: (i, 0)),
                      pl.BlockSpec((None, ff, tn), lambda i, j, pe: (expert(i, pe), 0, j))],
            out_specs=pl.BlockSpec((rows, tn), lambda i, j, pe: (i, j))),
        out_shape=jax.ShapeDtypeStruct((p, d), F32), compiler_params=_cparams(2))(pends, a, wd)


def _route_kernel(lg_ref, info_ref, w_ref, cnt_ref, carry_sc):
    i = pl.program_id(0)

    @pl.when(i == 0)
    def _():
        carry_sc[...] = jnp.zeros(carry_sc.shape, F32)

    tr = lg_ref.shape[0]
    lane = lax.broadcasted_iota(I32, (tr, LANES), 1)
    lg = jnp.where(lane < N_EXPERTS, lg_ref[...], -jnp.inf)
    m1 = jnp.max(lg, axis=1, keepdims=True)
    i1 = jnp.min(jnp.where(lg == m1, lane, LANES), axis=1, keepdims=True)
    lg2 = jnp.where(lane == i1, -jnp.inf, lg)
    m2 = jnp.max(lg2, axis=1, keepdims=True)
    i2 = jnp.min(jnp.where(lg2 == m2, lane, LANES), axis=1, keepdims=True)
    e = jnp.exp(m2 - m1)
    w1 = 1.0 / (1.0 + e)
    w2 = e * w1
    oh = jnp.where(jnp.logical_or(lane == i1, lane == i2), 1.0, 0.0)
    tri = jnp.where(lax.broadcasted_iota(I32, (tr, tr), 1) < lax.broadcasted_iota(I32, (tr, tr), 0), 1.0, 0.0)
    before = jnp.dot(tri.astype(BF16), oh.astype(BF16), preferred_element_type=F32) + carry_sc[0:1, :]
    r1 = jnp.sum(jnp.where(lane == i1, before, 0.0), axis=1, keepdims=True).astype(I32)
    r2 = jnp.sum(jnp.where(lane == i2, before, 0.0), axis=1, keepdims=True).astype(I32)
    carry_sc[0:1, :] = carry_sc[0:1, :] + jnp.sum(oh, axis=0, keepdims=True)
    info_ref[...] = jnp.where(lane == 0, i1, jnp.where(lane == 1, i2, jnp.where(lane == 2, r1, jnp.where(lane == 3, r2, 0))))
    w_ref[...] = jnp.where(lane == 0, w1, jnp.where(lane == 1, w2, 0.0))
    cnt_ref[...] = carry_sc[...]


def _route(logits):
    t = logits.shape[0]
    tr = _tile(t, 256, SUBLANES)
    blk = pl.BlockSpec((tr, LANES), lambda i: (i, 0))
    return pl.pallas_call(
        _route_kernel, grid=(t // tr,), in_specs=[blk],
        out_specs=[blk, blk, pl.BlockSpec((SUBLANES, LANES), lambda i: (0, 0))],
        out_shape=[jax.ShapeDtypeStruct((t, LANES), I32), jax.ShapeDtypeStruct((t, LANES), F32),
                   jax.ShapeDtypeStruct((SUBLANES, LANES), F32)],
        scratch_shapes=[pltpu.VMEM((SUBLANES, LANES), F32)], compiler_params=_cparams(1))(logits)


def _dispatch_kernel(e_ref, r_ref, ps_ref, h_ref, xz_hbm, xbuf_hbm, sem):
    del xz_hbm
    tr = h_ref.shape[0]
    base = pl.program_id(0) * tr

    def copy(r, k):
        a = (base + r) * 2 + k
        dst = ps_ref[e_ref[a]] + r_ref[a]
        return pltpu.make_async_copy(h_ref.at[pl.ds(r, 1)], xbuf_hbm.at[pl.ds(dst, 1)], sem)

    def start(r, carry):
        copy(r, 0).start()
        copy(r, 1).start()
        return carry
    lax.fori_loop(0, tr, start, 0)

    def wait(r, carry):
        copy(r, 0).wait()
        copy(r, 1).wait()
        return carry
    lax.fori_loop(0, tr, wait, 0)


def _dispatch(h, e_flat, r_flat, pstarts, p_rows):
    t, w = h.shape
    tr = _tile(t, 256, SUBLANES)
    grid_spec = pltpu.PrefetchScalarGridSpec(
        num_scalar_prefetch=3, grid=(t // tr,),
        in_specs=[pl.BlockSpec((tr, w), lambda i, *_: (i, 0)), pl.BlockSpec(memory_space=pl.ANY)],
        out_specs=pl.BlockSpec(memory_space=pl.ANY),
        scratch_shapes=[pltpu.SemaphoreType.DMA(())])
    return pl.pallas_call(
        _dispatch_kernel, grid_spec=grid_spec,
        out_shape=jax.ShapeDtypeStruct((p_rows, w), h.dtype),
        input_output_aliases={4: 0}, compiler_params=_cparams(1),
    )(e_flat, r_flat, pstarts, h, jnp.zeros((p_rows, w), h.dtype))


def _combine_kernel(e_ref, r_ref, ps_ref, x_ref, w_ref, mod_ref, gain_ref, y_hbm, o_ref, ybuf, sem, *, res_row):
    tr = x_ref.shape[0]
    base = pl.program_id(0) * tr

    def copy(r, k):
        a = (base + r) * 2 + k
        src = ps_ref[e_ref[a]] + r_ref[a]
        return pltpu.make_async_copy(y_hbm.at[pl.ds(src, 1)], ybuf.at[k, pl.ds(r, 1)], sem)

    def start(r, carry):
        copy(r, 0).start()
        copy(r, 1).start()
        return carry
    lax.fori_loop(0, tr, start, 0)

    def wait(r, carry):
        copy(r, 0).wait()
        copy(r, 1).wait()
        return carry
    lax.fori_loop(0, tr, wait, 0)

    w = w_ref[...]
    y = ybuf[0] * w[:, 0:1] + ybuf[1] * w[:, 1:2]
    x = x_ref[...] + mod_ref[res_row:res_row + 1, :] * y
    o_ref[...] = _rms(x, gain_ref[...])


def _combine_final(x, ybuf, wts, mod, gain, e_flat, r_flat, pstarts, seq, res_row):
    t, d = x.shape
    tr = _tile(seq, 256, SUBLANES)
    per_b = seq // tr
    grid_spec = pltpu.PrefetchScalarGridSpec(
        num_scalar_prefetch=3, grid=(t // tr,),
        in_specs=[pl.BlockSpec((tr, d), lambda i, *_: (i, 0)), pl.BlockSpec((tr, LANES), lambda i, *_: (i, 0)),
                  pl.BlockSpec((None, SUBLANES, d), lambda i, *_: (i // per_b, 0, 0)),
                  pl.BlockSpec((1, d), lambda i, *_: (0, 0)), pl.BlockSpec(memory_space=pl.ANY)],
        out_specs=pl.BlockSpec((tr, d), lambda i, *_: (i, 0)),
        scratch_shapes=[pltpu.VMEM((2, tr, d), F32), pltpu.SemaphoreType.DMA(())])
    return pl.pallas_call(
        functools.partial(_combine_kernel, res_row=res_row), grid_spec=grid_spec,
        out_shape=jax.ShapeDtypeStruct((t, d), F32), compiler_params=_cparams(1),
    )(e_flat, r_flat, pstarts, x, wts, mod, gain.reshape(1, d), ybuf)


def _split_w_in(w_in, d):
    sizes = (DIL_WIDTH, DIL_WIDTH, DIL_WIDTH, MLA_Q_LORA, MLA_KV_LORA, MLA_ROPE,
             MOBA_HEADS * HEAD_DIM, MOBA_HEADS * HEAD_DIM, MOBA_HEADS * HEAD_DIM,
             DSA_HEADS * HEAD_DIM, HEAD_DIM, HEAD_DIM, IDX_HEADS * IDX_DIM, IDX_DIM, IDX_HEADS, N_BRANCH * d)
    names = ("a_q", "a_k", "a_v", "b_cq", "b_ckv", "b_kr", "c_q", "c_k", "c_v",
             "d_q", "d_k", "d_v", "d_iq", "d_ik", "d_iw", "g")
    assert w_in.shape[1] == sum(sizes)
    parts, o = {}, 0
    for nme, sz in zip(names, sizes):
        parts[nme] = w_in[:, o:o + sz]
        o += sz
    return parts


def _cat_cols(parts, names, pad_to=None):
    cols, offs, o = [], {}, 0
    for nme in names:
        wpart = parts[nme]
        width = -(-wpart.shape[1] // LANES) * LANES
        if width != wpart.shape[1]:
            wpart = jnp.pad(wpart, ((0, 0), (0, width - wpart.shape[1])))
        cols.append(wpart.astype(BF16))
        offs[nme] = o // LANES
        o += width
    if pad_to is not None and o % pad_to:
        cols.append(jnp.zeros((cols[0].shape[0], pad_to - o % pad_to), BF16))
    return jnp.concatenate(cols, axis=1), offs


def _mixer(h, x, mod, bsz, seq, tables, w_in, q_norm, q_up, kv_norm, kv_up, w_branch, w_out):
    t, d = h.shape
    rope_h, rope_iq, rope_ik, rope_mla = tables
    parts = _split_w_in(w_in, d)
    w_rot, ro = _cat_cols(parts, ("c_q", "c_k", "d_q"))
    w_plain, po = _cat_cols(parts, ("b_cq", "c_v", "b_ckv", "d_v", "b_kr"), pad_to=512)
    assert ro["d_q"] * LANES % (DSA_HEADS * HEAD_DIM) == 0
    assert po["b_cq"] * LANES % MLA_Q_LORA == 0 and po["b_ckv"] * LANES % MLA_KV_LORA == 0

    rot = _matmul(h, w_rot, out_dtype=BF16, tn_pref=1024, epi="rope", rope=rope_h, half=ROT_DIM // 2)
    plain = _matmul(h, w_plain, out_dtype=BF16, tn_pref=512)
    d_k = _matmul(h, parts["d_k"].astype(BF16), out_dtype=BF16, tn_pref=LANES, epi="rope", rope=rope_h, half=ROT_DIM // 2)
    gw = DIL_OUT
    a_qk = [_matmul(h, jnp.concatenate([parts["a_q"][:, g * gw:(g + 1) * gw], parts["a_k"][:, g * gw:(g + 1) * gw]],
                                       axis=1).astype(BF16),
                    out_dtype=BF16, tn_pref=1024, epi="rope", rope=rope_h, half=ROT_DIM // 2)
            for g in range(len(DIL_GROUPS))]
    a_v = [_matmul(h, parts["a_v"][:, g * gw:(g + 1) * gw].astype(BF16), out_dtype=BF16, tn_pref=1024)
           for g in range(len(DIL_GROUPS))]
    iq = _matmul(h, parts["d_iq"].astype(BF16), out_dtype=BF16, tn_pref=1024, epi="rope", rope=rope_iq, half=IDX_ROT // 2)
    ik = _matmul(h, _cat_cols(parts, ("d_ik",))[0], out_dtype=F32, tn_pref=LANES, epi="rope", rope=rope_ik, half=IDX_ROT // 2)
    iw = _matmul(h, _cat_cols(parts, ("d_iw",))[0], out_dtype=F32, tn_pref=LANES)
    gates = _matmul(h, parts["g"].astype(BF16), out_dtype=BF16, tn_pref=1024, epi="sigmoid")

    o_a = _dilated_mixture(a_qk, a_v, bsz, seq)

    hq = MLA_NOPE + MLA_ROPE
    wq = q_up.reshape(MLA_Q_LORA, MLA_HEADS, hq)
    wq = jnp.pad(wq, ((0, 0), (0, 0), (0, MLA_QK_PAD - hq))).reshape(MLA_Q_LORA, MLA_HEADS * MLA_QK_PAD).astype(BF16)
    wkv = kv_up.reshape(MLA_KV_LORA, MLA_HEADS, MLA_NOPE + MLA_V)
    wkv = jnp.concatenate([wkv[:, :, :MLA_NOPE].reshape(MLA_KV_LORA, -1), wkv[:, :, MLA_NOPE:].reshape(MLA_KV_LORA, -1)],
                          axis=1).astype(BF16)
    q_cat, k_cat, v_b = _mla_project(plain, po["b_cq"], po["b_ckv"], po["b_kr"], q_norm, wq, kv_norm, wkv, rope_mla)
    o_b = _flash(q_cat, k_cat, v_b, bsz, seq, MLA_HEADS, MLA_QK_PAD, MLA_V, 0, 0, 0, hq ** -0.5)

    kmean = _moba_kmean(rot, bsz, seq, ro["c_k"])
    o_c = _flash(rot, rot, plain, bsz, seq, MOBA_HEADS, HEAD_DIM, HEAD_DIM, ro["c_q"], ro["c_k"], po["c_v"],
                 HEAD_DIM ** -0.5, kmean=kmean)

    o_d = _dsa_attention(rot, d_k, plain, iq, ik, iw, bsz, seq, ro["d_q"], po["d_v"])

    sizes = (DIL_OUT, MLA_HEADS * MLA_V, MOBA_HEADS * HEAD_DIM, DSA_HEADS * HEAD_DIM)
    projs, o = [], 0
    for sz in sizes:
        projs.append(w_branch[o:o + sz].astype(BF16))
        o += sz
    y = _branch_merge((o_a, o_b, o_c, o_d), projs, gates, d)
    return _matmul(y, w_out.astype(BF16), out_dtype=F32, tn_pref=512, epi="resid", xres=x, mod=mod, res_row=2, seq=seq)


def kernel(x, c, positions, w_ada, b_ada, ada_table_0, mix_norm_0, w_in_0, mla_q_norm_0, mla_q_up_0, mla_kv_norm_0, mla_kv_up_0, w_branch_0, w_out_0, ffn_norm_0, ffn_gate_0, ffn_up_0, ffn_down_0, ada_table_1, mix_norm_1, w_in_1, mla_q_norm_1, mla_q_up_1, mla_kv_norm_1, mla_kv_up_1, w_branch_1, w_out_1, ffn_norm_1, router_1, expert_gate_1, expert_up_1, expert_down_1, final_norm):
    bsz, seq, d = x.shape
    t = bsz * seq
    xf = x.reshape(t, d)
    mod0, mod1 = _ada_mod(c, w_ada, b_ada, ada_table_0, ada_table_1)

    pos_b = jnp.broadcast_to(positions.reshape(t, 1).astype(F32), (t, LANES))
    tables = (_rope_tables(pos_b, _rope_pattern(HEAD_DIM, ROT_DIM, LANES)),
              _rope_tables(pos_b, _rope_pattern(IDX_DIM, IDX_ROT, LANES)),
              _rope_tables(pos_b, _rope_pattern(IDX_DIM, IDX_ROT, IDX_DIM)),
              _rope_tables(pos_b, _rope_pattern(MLA_ROPE, MLA_ROPE, MLA_ROPE)))

    (h,) = _norm_mod(xf, mod0, mix_norm_0, seq, shift_row=0, scale_row=1)
    xf = _mixer(h, xf, mod0, bsz, seq, tables, w_in_0, mla_q_norm_0, mla_q_up_0, mla_kv_norm_0, mla_kv_up_0, w_branch_0, w_out_0)
    (h,) = _norm_mod(xf, mod0, ffn_norm_0, seq, shift_row=3, scale_row=4)
    y = _ffn_dense(h, ffn_gate_0.astype(BF16), ffn_up_0.astype(BF16), ffn_down_0.astype(BF16))

    xf, h = _norm_mod(xf, mod1, mix_norm_1, seq, shift_row=0, scale_row=1, y=y, res_mod=mod0, res_row=5)
    xf = _mixer(h, xf, mod1, bsz, seq, tables, w_in_1, mla_q_norm_1, mla_q_up_1, mla_kv_norm_1, mla_kv_up_1, w_branch_1, w_out_1)
    w_r = jnp.pad(router_1, ((0, 0), (0, LANES - N_EXPERTS))).astype(BF16)
    h, logits = _norm_mod(xf, mod1, ffn_norm_1, seq, shift_row=3, scale_row=4, w_router=w_r, h_dtype=U32)
    info, wts, cnt = _route(logits)
    e_flat = info[:, 0:2].reshape(2 * t)
    r_flat = info[:, 2:4].reshape(2 * t)
    counts = cnt[0, :N_EXPERTS].astype(I32)
    rows = min(MOE_ROWS, t)
    padded = (counts + rows - 1) // rows * rows
    pends = jnp.cumsum(padded).astype(I32)
    pstarts = pends - padded
    p_rows = (-(-2 * t // rows) + N_EXPERTS) * rows
    xbuf = _dispatch(h, e_flat, r_flat, pstarts, p_rows)
    ybuf = _ffn_experts(xbuf, pends, expert_gate_1.astype(BF16), expert_up_1.astype(BF16), expert_down_1.astype(BF16), rows)
    out = _combine_final(xf, ybuf, wts, mod1, final_norm, e_flat, r_flat, pstarts, seq, res_row=5)
    return out.reshape(bsz, seq, d)
```

```python
import functools

import numpy as np
import jax
import jax.numpy as jnp
from jax import lax
from jax.experimental import pallas as pl
from jax.experimental.pallas import tpu as pltpu

F32, BF16, I32, U32 = jnp.float32, jnp.bfloat16, jnp.int32, jnp.uint32

LANES = 128
SUBLANES = 8
VMEM_LIMIT_BYTES = 56 * 1024 * 1024

HEAD_DIM = 128
ROT_DIM = HEAD_DIM // 4
ROPE_THETA = 500000.0
NORM_EPS = 1e-6
MASK_VALUE = -1e30
DIL_GROUPS = ((128, 1), (512, 4), (2048, 16))
DIL_HEADS_PER_GROUP = 4
DIL_HEADS = len(DIL_GROUPS) * DIL_HEADS_PER_GROUP
DIL_WIDTH = DIL_HEADS * HEAD_DIM
DIL_OUT = DIL_HEADS_PER_GROUP * HEAD_DIM
MLA_HEADS = 8
MLA_Q_LORA = 1536
MLA_KV_LORA = 512
MLA_NOPE = 128
MLA_ROPE = 64
MLA_V = 128
MLA_QK_PAD = 256
MOBA_HEADS = 8
MOBA_BLOCK = 256
MOBA_TOPK = 3
DSA_HEADS = 8
DSA_TOPK = 256
IDX_HEADS = 32
IDX_DIM = 64
IDX_ROT = IDX_DIM // 4
N_BRANCH = 4
N_EXPERTS = 8
N_MOD = 6
INT_MIN = -2147483648

DIL_RUN = 4
DIAG_ROWS = 512
DSA_TQ = 512
DSA_TK = 512
COUNT_ROWS = 64
MOE_ROWS = 512


def _cparams(n_axes):
    return pltpu.CompilerParams(dimension_semantics=("arbitrary",) * n_axes,
                                vmem_limit_bytes=VMEM_LIMIT_BYTES)


def _tile(n, pref, mult=LANES):
    if n <= pref:
        return n
    t = (pref // mult) * mult
    while t > mult and n % t:
        t -= mult
    assert n % t == 0, (n, pref, mult)
    return t


def _nt_dot(a, b):
    return lax.dot_general(a, b, (((1,), (1,)), ((), ())), preferred_element_type=F32)


def _pack_bf16_pairs(x):
    n = x.shape[1]
    bits = lax.bitcast_convert_type(x.astype(BF16).astype(F32), U32)
    return (bits[:, n // 2:] & jnp.uint32(0xFFFF0000)) | (bits[:, :n // 2] >> 16)


def _unpack_bf16_pairs(w):
    lo = lax.bitcast_convert_type(w << 16, F32).astype(BF16)
    hi = lax.bitcast_convert_type(w & jnp.uint32(0xFFFF0000), F32).astype(BF16)
    return lo, hi


def _rope_pattern(group, rot, active):
    half = rot // 2
    freqs = ROPE_THETA ** (-jnp.arange(half, dtype=F32) * (2.0 / rot))
    lane = np.arange(LANES)
    p = lane % group
    on = (lane < active) & (p < rot)
    idx = np.where(on, p % half, 0)
    freq = jnp.where(jnp.asarray(on), freqs[idx], 0.0)
    first = (on & (p < half)).astype(np.float32)
    second = (on & (p >= half)).astype(np.float32)
    pat = jnp.zeros((SUBLANES, LANES), F32)
    return pat.at[0].set(freq).at[1].set(jnp.asarray(first)).at[2].set(jnp.asarray(second))


def _rope_table_kernel(pos_ref, pat_ref, c_ref, sa_ref, sb_ref):
    ang = pos_ref[...] * pat_ref[0:1, :]
    c_ref[...] = jnp.cos(ang)
    sin = jnp.sin(ang)
    sa_ref[...] = -sin * pat_ref[1:2, :]
    sb_ref[...] = sin * pat_ref[2:3, :]


def _rope_tables(pos_b, pat):
    t = pos_b.shape[0]
    ts = _tile(t, 1024)
    spec = pl.BlockSpec((ts, LANES), lambda i: (i, 0))
    sds = jax.ShapeDtypeStruct((t, LANES), F32)
    return pl.pallas_call(
        _rope_table_kernel, grid=(t // ts,),
        in_specs=[spec, pl.BlockSpec((SUBLANES, LANES), lambda i: (0, 0))],
        out_specs=[spec, spec, spec], out_shape=[sds, sds, sds],
        compiler_params=_cparams(1))(pos_b, pat)


def _apply_rope(x, c, sa, sb, half):
    return x * c + pltpu.roll(x, LANES - half, 1) * sa + pltpu.roll(x, half, 1) * sb


def _ada_kernel(c_ref, w_ref, b_ref, t0_ref, t1_ref, o0_ref, o1_ref):
    c = c_ref[...]
    a = (c * jax.nn.sigmoid(c)).astype(BF16)
    acc = jnp.dot(a, w_ref[...].astype(BF16), preferred_element_type=F32) + b_ref[...]
    o0_ref[...] = acc + t0_ref[...]
    o1_ref[...] = acc + t1_ref[...]


def _ada_mod(c, w_ada, b_ada, table0, table1):
    b, d = c.shape
    n = w_ada.shape[1]
    bp = -(-b // SUBLANES) * SUBLANES
    cp = jnp.zeros((bp, d), F32).at[:b].set(c)
    tn = _tile(n, 512)
    row = pl.BlockSpec((1, tn), lambda j: (0, j))
    out = pl.BlockSpec((bp, tn), lambda j: (0, j))
    sds = jax.ShapeDtypeStruct((bp, n), F32)
    m0, m1 = pl.pallas_call(
        _ada_kernel, grid=(n // tn,),
        in_specs=[pl.BlockSpec((bp, d), lambda j: (0, 0)), pl.BlockSpec((d, tn), lambda j: (0, j)), row, row, row],
        out_specs=[out, out], out_shape=[sds, sds], compiler_params=_cparams(1),
    )(cp, w_ada, b_ada.reshape(1, n), table0.reshape(1, n), table1.reshape(1, n))

    def pack(m):
        m = m[:b].reshape(b, N_MOD, d)
        return jnp.concatenate([m, jnp.zeros((b, SUBLANES - N_MOD, d), F32)], axis=1)
    return pack(m0), pack(m1)


def _norm_kernel(*refs, res_row, shift_row, scale_row, with_logits, h_dtype):
    it = iter(refs)
    x_ref = next(it)
    y_ref, rmod_ref = (next(it), next(it)) if res_row is not None else (None, None)
    mod_ref, gain_ref = next(it), next(it)
    wr_ref = next(it) if with_logits else None
    xo_ref = next(it) if res_row is not None else None
    h_ref = next(it)
    lg_ref = next(it) if with_logits else None

    x = x_ref[...]
    if res_row is not None:
        x = x + rmod_ref[res_row:res_row + 1, :] * y_ref[...].astype(F32)
        xo_ref[...] = x
    y = x * lax.rsqrt(jnp.mean(x * x, axis=-1, keepdims=True) + NORM_EPS) * gain_ref[...]
    h = y * (1.0 + mod_ref[scale_row:scale_row + 1, :]) + mod_ref[shift_row:shift_row + 1, :]
    if h_dtype == U32:
        h_ref[...] = _pack_bf16_pairs(h)
    else:
        h_ref[...] = h.astype(h_dtype)
    if with_logits:
        lg_ref[...] = jnp.dot(h.astype(BF16), wr_ref[...], preferred_element_type=F32)


def _norm_mod(x, mod, gain, seq, *, shift_row, scale_row, y=None, res_mod=None, res_row=None, w_router=None,
              h_dtype=BF16):
    t, d = x.shape
    ts = _tile(seq, 256, SUBLANES)
    per_b = seq // ts
    blk = pl.BlockSpec((ts, d), lambda i: (i, 0))
    mod_spec = pl.BlockSpec((None, SUBLANES, d), lambda i: (i // per_b, 0, 0))
    in_specs, args = [blk], [x]
    if res_row is not None:
        in_specs += [blk, mod_spec]
        args += [y, res_mod]
    in_specs += [mod_spec, pl.BlockSpec((1, d), lambda i: (0, 0))]
    args += [mod, gain.reshape(1, d)]
    if w_router is not None:
        in_specs.append(pl.BlockSpec((d, LANES), lambda i: (0, 0)))
        args.append(w_router)
    out_specs, out_shape = [], []
    if res_row is not None:
        out_specs.append(blk)
        out_shape.append(jax.ShapeDtypeStruct((t, d), F32))
    hd = d // 2 if h_dtype == U32 else d
    out_specs.append(pl.BlockSpec((ts, hd), lambda i: (i, 0)))
    out_shape.append(jax.ShapeDtypeStruct((t, hd), h_dtype))
    if w_router is not None:
        out_specs.append(pl.BlockSpec((ts, LANES), lambda i: (i, 0)))
        out_shape.append(jax.ShapeDtypeStruct((t, LANES), F32))
    kern = functools.partial(_norm_kernel, res_row=res_row, shift_row=shift_row, scale_row=scale_row,
                             with_logits=w_router is not None, h_dtype=h_dtype)
    return pl.pallas_call(kern, grid=(t // ts,), in_specs=in_specs, out_specs=out_specs, out_shape=out_shape,
                          compiler_params=_cparams(1))(*args)


def _mm_kernel(*refs, epi, half, res_row):
    a_ref, w_ref = refs[0], refs[1]
    o_ref = refs[-1]
    acc = jnp.dot(a_ref[...], w_ref[...], preferred_element_type=F32)
    if epi == "plain":
        o_ref[...] = acc.astype(o_ref.dtype)
    elif epi == "sigmoid":
        o_ref[...] = jax.nn.sigmoid(acc).astype(o_ref.dtype)
    elif epi == "rope":
        c, sa, sb = refs[2][...], refs[3][...], refs[4][...]
        for ch in range(acc.shape[1] // LANES):
            sl = slice(ch * LANES, (ch + 1) * LANES)
            o_ref[:, sl] = _apply_rope(acc[:, sl], c, sa, sb, half).astype(o_ref.dtype)
    elif epi == "resid":
        xres_ref, mod_ref = refs[2], refs[3]
        o_ref[...] = xres_ref[...] + mod_ref[res_row:res_row + 1, :] * acc
    else:
        raise ValueError(epi)


def _matmul(a, w, *, out_dtype, tn_pref, epi="plain", rope=None, half=0, xres=None, mod=None, res_row=None, seq=None):
    m, k = a.shape
    n = w.shape[1]
    tm = _tile(seq if seq is not None else m, 1024, SUBLANES)
    tn = _tile(n, tn_pref)
    in_specs = [pl.BlockSpec((tm, k), lambda i, j: (i, 0)), pl.BlockSpec((k, tn), lambda i, j: (0, j))]
    args = [a, w]
    if epi == "rope":
        in_specs += [pl.BlockSpec((tm, LANES), lambda i, j: (i, 0))] * 3
        args += list(rope)
    if epi == "resid":
        per_b = seq // tm
        in_specs += [pl.BlockSpec((tm, tn), lambda i, j: (i, j)),
                     pl.BlockSpec((None, SUBLANES, tn), lambda i, j: (i // per_b, 0, j))]
        args += [xres, mod]
    kern = functools.partial(_mm_kernel, epi=epi, half=half, res_row=res_row)
    return pl.pallas_call(
        kern, grid=(m // tm, n // tn), in_specs=in_specs,
        out_specs=pl.BlockSpec((tm, tn), lambda i, j: (i, j)),
        out_shape=jax.ShapeDtypeStruct((m, n), out_dtype), compiler_params=_cparams(2))(*args)


def _dil_kernel(q_ref, kp_ref, ko_ref, vp_ref, vo_ref, o_ref, lse_ref, *, scale):
    i = pl.program_id(2)
    w = kp_ref.shape[0]
    row = lax.broadcasted_iota(I32, (w, w), 0)
    col = lax.broadcasted_iota(I32, (w, w), 1)
    for j in range(q_ref.shape[0] // w):
        rs = slice(j * w, (j + 1) * w)
        ps = slice((j - 1) * w, j * w)
        for h in range(DIL_HEADS_PER_GROUP):
            sl = slice(h * HEAD_DIM, (h + 1) * HEAD_DIM)
            q = q_ref[rs, sl]
            k_prev = kp_ref[:, sl] if j == 0 else ko_ref[ps, sl]
            v_prev = vp_ref[:, sl] if j == 0 else vo_ref[ps, sl]
            s_o = jnp.where(col <= row, _nt_dot(q, ko_ref[rs, sl]) * scale, MASK_VALUE)
            s_p = jnp.where(col >= row, _nt_dot(q, k_prev) * scale, MASK_VALUE)
            if j == 0:
                s_p = jnp.where(i > 0, s_p, MASK_VALUE)
            m = jnp.maximum(jnp.max(s_o, axis=1, keepdims=True), jnp.max(s_p, axis=1, keepdims=True))
            p_o = jnp.exp(s_o - m)
            p_p = jnp.exp(s_p - m)
            l = jnp.sum(p_o, axis=1, keepdims=True) + jnp.sum(p_p, axis=1, keepdims=True)
            o = (jnp.dot(p_o.astype(BF16), vo_ref[rs, sl], preferred_element_type=F32)
                 + jnp.dot(p_p.astype(BF16), v_prev, preferred_element_type=F32))
            o_ref[rs, sl] = o / l
            lse_ref[rs, sl] = jnp.broadcast_to(m + jnp.log(l), o.shape)


def _dilated_group(qk, v, bsz, seq, g):
    window, dil = DIL_GROUPS[g]
    wsub = window // dil
    n = seq // dil
    assert wsub == HEAD_DIM and n % wsub == 0
    nb = n // wsub
    qk_v = qk.reshape(bsz, n, dil * 2 * DIL_OUT)
    v_v = v.reshape(bsz, n, dil * DIL_OUT)
    run = min(DIL_RUN, nb)
    assert nb % run == 0
    blk, pblk = (None, run * wsub, DIL_OUT), (None, wsub, DIL_OUT)
    q_spec = pl.BlockSpec(blk, lambda b, r, i: (b, i, 2 * r))
    k_own = pl.BlockSpec(blk, lambda b, r, i: (b, i, 2 * r + 1))
    k_prev = pl.BlockSpec(pblk, lambda b, r, i: (b, jnp.maximum(i * run - 1, 0), 2 * r + 1))
    v_own = pl.BlockSpec(blk, lambda b, r, i: (b, i, r))
    v_prev = pl.BlockSpec(pblk, lambda b, r, i: (b, jnp.maximum(i * run - 1, 0), r))
    sds = jax.ShapeDtypeStruct((bsz, n, dil * DIL_OUT), F32)
    o, lse = pl.pallas_call(
        functools.partial(_dil_kernel, scale=HEAD_DIM ** -0.5), grid=(bsz, dil, nb // run),
        in_specs=[q_spec, k_prev, k_own, v_prev, v_own],
        out_specs=[v_own, v_own], out_shape=[sds, sds], compiler_params=_cparams(3),
    )(qk_v, qk_v, qk_v, v_v, v_v)
    return o.reshape(bsz * seq, DIL_OUT), lse.reshape(bsz * seq, DIL_OUT)


def _dil_combine_kernel(o0, o1, o2, l0, l1, l2, out_ref):
    a, b, c = l0[...], l1[...], l2[...]
    m = jnp.maximum(jnp.maximum(a, b), c)
    ea, eb, ec = jnp.exp(a - m), jnp.exp(b - m), jnp.exp(c - m)
    out_ref[...] = ((ea * o0[...] + eb * o1[...] + ec * o2[...]) / (ea + eb + ec)).astype(out_ref.dtype)


def _dilated_mixture(qks, vs, bsz, seq):
    outs, lses = zip(*[_dilated_group(qks[g], vs[g], bsz, seq, g) for g in range(len(DIL_GROUPS))])
    t = bsz * seq
    ts = _tile(t, 1024, SUBLANES)
    spec = pl.BlockSpec((ts, DIL_OUT), lambda i: (i, 0))
    return pl.pallas_call(
        _dil_combine_kernel, grid=(t // ts,), in_specs=[spec] * 6, out_specs=spec,
        out_shape=jax.ShapeDtypeStruct((t, DIL_OUT), BF16), compiler_params=_cparams(1))(*outs, *lses)


def _rms(x, gain):
    return x * lax.rsqrt(jnp.mean(x * x, axis=-1, keepdims=True) + NORM_EPS) * gain


def _mla_q_kernel(cq_ref, gain_ref, w_ref, c_ref, sa_ref, sb_ref, o_ref):
    cqn = _rms(cq_ref[...].astype(F32), gain_ref[...]).astype(BF16)
    acc = jnp.dot(cqn, w_ref[...], preferred_element_type=F32)
    c, sa, sb = c_ref[...], sa_ref[...], sb_ref[...]
    for ch in range(acc.shape[1] // LANES):
        sl = slice(ch * LANES, (ch + 1) * LANES)
        x = acc[:, sl]
        if ch % 2 == 1:
            x = _apply_rope(x, c, sa, sb, MLA_ROPE // 2)
        o_ref[:, sl] = x.astype(o_ref.dtype)


def _mla_kv_kernel(ckv_ref, gain_ref, w_ref, kr_ref, c_ref, sa_ref, sb_ref, k_ref, v_ref):
    ckvn = _rms(ckv_ref[...].astype(F32), gain_ref[...]).astype(BF16)
    acc = jnp.dot(ckvn, w_ref[...], preferred_element_type=F32)
    kpe = _apply_rope(kr_ref[...].astype(F32), c_ref[...], sa_ref[...], sb_ref[...], MLA_ROPE // 2).astype(k_ref.dtype)
    hw = MLA_HEADS * MLA_NOPE
    for h in range(MLA_HEADS):
        k_ref[:, h * MLA_QK_PAD:h * MLA_QK_PAD + MLA_NOPE] = acc[:, h * MLA_NOPE:(h + 1) * MLA_NOPE].astype(k_ref.dtype)
        k_ref[:, h * MLA_QK_PAD + MLA_NOPE:(h + 1) * MLA_QK_PAD] = kpe
    v_ref[...] = acc[:, hw:].astype(v_ref.dtype)


def _mla_project(plain, cq_col, ckv_col, kr_col, q_gain, wq, kv_gain, wkv, rope):
    t = plain.shape[0]
    ts = _tile(t, 512, SUBLANES)
    tab = pl.BlockSpec((ts, LANES), lambda i: (i, 0))
    nq = MLA_HEADS * MLA_QK_PAD
    q_cat = pl.pallas_call(
        _mla_q_kernel, grid=(t // ts,),
        in_specs=[pl.BlockSpec((ts, MLA_Q_LORA), lambda i: (i, cq_col * LANES // MLA_Q_LORA)),
                  pl.BlockSpec((1, MLA_Q_LORA), lambda i: (0, 0)),
                  pl.BlockSpec((MLA_Q_LORA, nq), lambda i: (0, 0)), tab, tab, tab],
        out_specs=pl.BlockSpec((ts, nq), lambda i: (i, 0)),
        out_shape=jax.ShapeDtypeStruct((t, nq), BF16), compiler_params=_cparams(1),
    )(plain, q_gain.reshape(1, -1), wq, *rope)
    nv = MLA_HEADS * MLA_V
    k_cat, v = pl.pallas_call(
        _mla_kv_kernel, grid=(t // ts,),
        in_specs=[pl.BlockSpec((ts, MLA_KV_LORA), lambda i: (i, ckv_col * LANES // MLA_KV_LORA)),
                  pl.BlockSpec((1, MLA_KV_LORA), lambda i: (0, 0)),
                  pl.BlockSpec((MLA_KV_LORA, MLA_HEADS * MLA_NOPE + nv), lambda i: (0, 0)),
                  pl.BlockSpec((ts, LANES), lambda i: (i, kr_col)), tab, tab, tab],
        out_specs=[pl.BlockSpec((ts, nq), lambda i: (i, 0)), pl.BlockSpec((ts, nv), lambda i: (i, 0))],
        out_shape=[jax.ShapeDtypeStruct((t, nq), BF16), jax.ShapeDtypeStruct((t, nv), BF16)],
        compiler_params=_cparams(1),
    )(plain, kv_gain.reshape(1, -1), wkv, plain, *rope)
    return q_cat, k_cat, v


def _softmax_step(s, v, m_sc, l_sc, acc_sc, rows):
    tiles = [s[:, j * LANES:(j + 1) * LANES] for j in range(s.shape[1] // LANES)]
    m_prev = m_sc[rows, :]
    m_new = jnp.maximum(m_prev, jnp.max(functools.reduce(jnp.maximum, tiles), axis=1, keepdims=True))
    alpha = jnp.exp(m_prev - m_new)
    ps = [jnp.exp(x - m_new) for x in tiles]
    l_sc[rows, :] = alpha * l_sc[rows, :] + jnp.sum(functools.reduce(jnp.add, ps), axis=1, keepdims=True)
    p = jnp.concatenate([x.astype(BF16) for x in ps], axis=1)
    acc_sc[rows, :] = alpha * acc_sc[rows, :] + jnp.dot(p, v, preferred_element_type=F32)
    m_sc[rows, :] = m_new


def _flash_kernel(*refs, scale, moba):
    if moba:
        qi_ref, st_ref, q_ref, k_ref, v_ref, km_ref, o_ref, m_sc, l_sc, acc_sc, sel_sc = refs
    else:
        qi_ref, st_ref, q_ref, k_ref, v_ref, o_ref, m_sc, l_sc, acc_sc = refs
    qi, step = qi_ref[pl.program_id(2)], st_ref[pl.program_id(2)]
    kv = qi - step
    tq, tk = q_ref.shape[0], k_ref.shape[0]

    @pl.when(step == 0)
    def _init():
        m_sc[...] = jnp.full(m_sc.shape, MASK_VALUE, F32)
        l_sc[...] = jnp.zeros(l_sc.shape, F32)
        acc_sc[...] = jnp.zeros(acc_sc.shape, F32)
        if moba:
            lane = lax.broadcasted_iota(I32, (tq, LANES), 1)
            own = (qi * tq + lax.broadcasted_iota(I32, (tq, LANES), 0)) // MOBA_BLOCK
            valid = lane < own
            g = jnp.where(valid, _nt_dot(q_ref[...], km_ref[...]), MASK_VALUE)
            sel = jnp.zeros((tq, LANES), F32)
            lane_f = lane.astype(F32)
            for _ in range(MOBA_TOPK):
                mx = jnp.max(g, axis=1, keepdims=True)
                first = jnp.min(jnp.where(g == mx, lane_f, float(LANES)), axis=1, keepdims=True)
                pick = lane_f == first
                sel = jnp.where(pick, 1.0, sel)
                g = jnp.where(pick, -jnp.inf, g)
            sel_sc[...] = jnp.where(valid, sel, 0.0)

    def block_flags(rows, n_rows, n_keys):
        lane = lax.broadcasted_iota(I32, (n_rows, LANES), 1)
        sel = sel_sc[rows, :]
        nsub = tk // MOBA_BLOCK
        flags = []
        for c in range(n_keys // MOBA_BLOCK):
            f = jnp.sum(jnp.where(lane == kv * nsub + c, sel, 0.0), axis=1, keepdims=True)
            flags.append(jnp.broadcast_to(f, (n_rows, MOBA_BLOCK)))
        return flags[0] if len(flags) == 1 else jnp.concatenate(flags, axis=1)

    @pl.when(step == 0)
    def _diagonal():
        dc = min(DIAG_ROWS, tq)
        for r0 in range(0, tq, dc):
            rows, kw = slice(r0, r0 + dc), r0 + dc
            s = _nt_dot(q_ref[rows, :], k_ref[0:kw, :]) * scale
            row = r0 + lax.broadcasted_iota(I32, (dc, kw), 0)
            col = lax.broadcasted_iota(I32, (dc, kw), 1)
            ok = col <= row
            if moba:
                same = (col // MOBA_BLOCK) == (row // MOBA_BLOCK)
                ok = jnp.logical_and(ok, jnp.logical_or(same, block_flags(rows, dc, kw) > 0.0))
            _softmax_step(jnp.where(ok, s, MASK_VALUE), v_ref[0:kw, :], m_sc, l_sc, acc_sc, rows)

    @pl.when(step > 0)
    def _past():
        s = _nt_dot(q_ref[...], k_ref[...]) * scale
        if moba:
            s = jnp.where(block_flags(slice(None), tq, tk) > 0.0, s, MASK_VALUE)
        _softmax_step(s, v_ref[...], m_sc, l_sc, acc_sc, slice(None))

    @pl.when(step == qi)
    def _fin():
        o_ref[...] = (acc_sc[...] / l_sc[:, 0:1]).astype(o_ref.dtype)


def _flash(q, k, v, bsz, seq, heads, dq, dv, q_col, k_col, v_col, scale, kmean=None):
    t = bsz * seq
    tq = _tile(seq, 1024, MOBA_BLOCK)
    nq = seq // tq
    pairs = [(i, s) for i in range(nq) for s in range(i + 1)]
    qi_arr = jnp.asarray([p[0] for p in pairs], I32)
    st_arr = jnp.asarray([p[1] for p in pairs], I32)
    in_specs = [pl.BlockSpec((tq, dq), lambda b, h, p, qi, st: (b * nq + qi[p], q_col + h)),
                pl.BlockSpec((tq, dq), lambda b, h, p, qi, st: (b * nq + qi[p] - st[p], k_col + h)),
                pl.BlockSpec((tq, dv), lambda b, h, p, qi, st: (b * nq + qi[p] - st[p], v_col + h))]
    args = [q, k, v]
    scratch = [pltpu.VMEM((tq, LANES), F32), pltpu.VMEM((tq, LANES), F32), pltpu.VMEM((tq, dv), F32)]
    if kmean is not None:
        in_specs.append(pl.BlockSpec((None, None, LANES, HEAD_DIM), lambda b, h, p, qi, st: (b, h, 0, 0)))
        args.append(kmean)
        scratch.append(pltpu.VMEM((tq, LANES), F32))
    grid_spec = pltpu.PrefetchScalarGridSpec(
        num_scalar_prefetch=2, grid=(bsz, heads, len(pairs)), in_specs=in_specs,
        out_specs=pl.BlockSpec((tq, dv), lambda b, h, p, qi, st: (b * nq + qi[p], h)),
        scratch_shapes=scratch)
    return pl.pallas_call(
        functools.partial(_flash_kernel, scale=scale, moba=kmean is not None), grid_spec=grid_spec,
        out_shape=jax.ShapeDtypeStruct((t, heads * dv), BF16), compiler_params=_cparams(3))(qi_arr, st_arr, *args)


def _kmean_kernel(k_ref, o_ref):
    s = k_ref.shape[0]
    nb = s // MOBA_BLOCK
    k = k_ref[...].astype(F32).reshape(nb, MOBA_BLOCK, HEAD_DIM)
    mean = jnp.sum(k, axis=1) * (1.0 / MOBA_BLOCK)
    full = jnp.concatenate([mean, jnp.zeros((LANES - nb, HEAD_DIM), F32)], axis=0)
    o_ref[...] = full.astype(o_ref.dtype)


def _moba_kmean(rot, bsz, seq, k_col):
    assert seq % MOBA_BLOCK == 0 and seq // MOBA_BLOCK <= LANES
    return pl.pallas_call(
        _kmean_kernel, grid=(bsz, MOBA_HEADS),
        in_specs=[pl.BlockSpec((seq, HEAD_DIM), lambda b, h: (b, k_col + h))],
        out_specs=pl.BlockSpec((None, None, LANES, HEAD_DIM), lambda b, h: (b, h, 0, 0)),
        out_shape=jax.ShapeDtypeStruct((bsz, MOBA_HEADS, LANES, HEAD_DIM), BF16),
        compiler_params=_cparams(2))(rot)


def _dsa_kernel(q_ref, iq_ref, iw_ref, ikbd_ref, k_ref, v_ref, o_ref, key_sc, cand_sc, cnt_sc, qs_sc, m_sc, l_sc, acc_sc, *, n_keep, scale,
                wscale):
    c = pl.program_id(1)
    tq, tk = DSA_TQ, DSA_TK
    nkt = (c * tq) // tk + 1
    rowg = c * tq + lax.broadcasted_iota(I32, (tq, tk), 0)
    col = lax.broadcasted_iota(I32, (tq, tk), 1)
    w = iw_ref[...] * wscale

    def score_tile(kt, carry):
        ikb = ikbd_ref[kt]
        sc = jnp.zeros((tq, tk), F32)
        for j in range(IDX_HEADS // 2):
            rel = jnp.maximum(_nt_dot(iq_ref[:, j * LANES:(j + 1) * LANES], ikb), 0.0)
            sc = sc + rel[:, :tk] * w[:, 2 * j:2 * j + 1] + rel[:, tk:] * w[:, 2 * j + 1:2 * j + 2]
        bits = lax.bitcast_convert_type(sc, I32)
        key = bits ^ ((bits >> 31) & 0x7FFFFFFF)
        key_sc[kt] = jnp.where(kt * tk + col <= rowg, key, INT_MIN)
        return carry
    lax.fori_loop(0, nkt, score_tile, 0)

    def count_ge(cand):
        cand_sc[...] = cand
        cnt_sc[...] = jnp.zeros(cnt_sc.shape, F32)

        def body(kt, carry):
            for r0 in range(0, tq, COUNT_ROWS):
                rows = slice(r0, r0 + COUNT_ROWS)
                c = cand_sc[rows, :]
                a = cnt_sc[rows, :]
                for j in range(tk // LANES):
                    a = a + jnp.where(key_sc[kt, rows, j * LANES:(j + 1) * LANES] >= c, 1.0, 0.0)
                cnt_sc[rows, :] = a
            return carry
        lax.fori_loop(0, nkt, body, 0)
        return jnp.broadcast_to(jnp.sum(cnt_sc[...], axis=1, keepdims=True), (tq, LANES))

    keep = jnp.float32(n_keep)
    thr = jnp.where(count_ge(jnp.zeros((tq, LANES), I32)) >= keep, 0, INT_MIN).astype(I32)

    def search(it, thr):
        cand = thr | (jnp.int32(1) << (30 - it))
        return jnp.where(count_ge(cand) >= keep, cand, thr)
    thr = lax.fori_loop(0, 31, search, thr)
    thr = jnp.maximum(thr, INT_MIN + 1)

    for h in range(DSA_HEADS):
        qs_sc[h * tq:(h + 1) * tq, :] = q_ref[:, h * HEAD_DIM:(h + 1) * HEAD_DIM]
    m_sc[...] = jnp.full(m_sc.shape, MASK_VALUE, F32)
    l_sc[...] = jnp.zeros(l_sc.shape, F32)
    acc_sc[...] = jnp.zeros(acc_sc.shape, F32)

    def attend(kt, carry):
        sel = jnp.concatenate([key_sc[kt, :, j * LANES:(j + 1) * LANES] >= thr for j in range(tk // LANES)], axis=1)
        ks = k_ref[pl.ds(pl.multiple_of(kt * tk, tk), tk), :]
        vs = v_ref[pl.ds(pl.multiple_of(kt * tk, tk), tk), :]
        s = _nt_dot(qs_sc[...], ks) * scale
        s = jnp.concatenate([jnp.where(sel, s[h * tq:(h + 1) * tq, :], MASK_VALUE) for h in range(DSA_HEADS)], axis=0)
        _softmax_step(s, vs, m_sc, l_sc, acc_sc, slice(None))
        return carry
    lax.fori_loop(0, nkt, attend, 0)

    for h in range(DSA_HEADS):
        rows = slice(h * tq, (h + 1) * tq)
        o_ref[:, h * HEAD_DIM:(h + 1) * HEAD_DIM] = (acc_sc[rows, :] / l_sc[rows, 0:1]).astype(o_ref.dtype)


def _dsa_attention(rot, k, plain, iq, ik, iw, bsz, seq, q_col, v_col):
    t = bsz * seq
    assert seq % DSA_TK == 0
    nkt = seq // DSA_TK
    nch = seq // DSA_TQ
    ika = ik.astype(BF16).reshape(bsz, nkt, DSA_TK, LANES)
    ikbd = jnp.concatenate([ika, jnp.roll(ika, IDX_DIM, axis=-1)], axis=2)
    n_keep = min(DSA_TOPK, seq // 4)
    qw = DSA_HEADS * HEAD_DIM
    iqw = IDX_HEADS * IDX_DIM
    kern = functools.partial(_dsa_kernel, n_keep=n_keep, scale=HEAD_DIM ** -0.5,
                             wscale=(IDX_HEADS ** -0.5) * (IDX_DIM ** -0.5))
    return pl.pallas_call(
        kern, grid=(bsz, nch),
        in_specs=[pl.BlockSpec((DSA_TQ, qw), lambda b, c: (b * nch + c, q_col * HEAD_DIM // qw)),
                  pl.BlockSpec((DSA_TQ, iqw), lambda b, c: (b * nch + c, 0)),
                  pl.BlockSpec((DSA_TQ, LANES), lambda b, c: (b * nch + c, 0)),
                  pl.BlockSpec((None, nkt, 2 * DSA_TK, LANES), lambda b, c: (b, 0, 0, 0)),
                  pl.BlockSpec((seq, HEAD_DIM), lambda b, c: (b, 0)),
                  pl.BlockSpec((seq, HEAD_DIM), lambda b, c: (b, v_col))],
        out_specs=pl.BlockSpec((DSA_TQ, qw), lambda b, c: (b * nch + c, 0)),
        out_shape=jax.ShapeDtypeStruct((t, qw), BF16),
        scratch_shapes=[pltpu.VMEM((nkt, DSA_TQ, DSA_TK), I32), pltpu.VMEM((DSA_TQ, LANES), I32),
                        pltpu.VMEM((DSA_TQ, LANES), F32), pltpu.VMEM((DSA_HEADS * DSA_TQ, HEAD_DIM), BF16),
                        pltpu.VMEM((DSA_HEADS * DSA_TQ, LANES), F32), pltpu.VMEM((DSA_HEADS * DSA_TQ, LANES), F32),
                        pltpu.VMEM((DSA_HEADS * DSA_TQ, HEAD_DIM), F32)],
        compiler_params=_cparams(2))(rot, iq, iw, ikbd, k, plain)


def _branch_kernel(oa, ob, oc, od, pa, pb, pc, pd, ga, gb, gc, gd, y_ref):
    y = None
    for o, p, g in ((oa, pa, ga), (ob, pb, gb), (oc, pc, gc), (od, pd, gd)):
        term = g[...].astype(F32) * jnp.dot(o[...], p[...], preferred_element_type=F32)
        y = term if y is None else y + term
    y_ref[...] = y.astype(y_ref.dtype)


def _branch_merge(outs, projs, gates, d):
    t = outs[0].shape[0]
    tm = _tile(t, 1024, SUBLANES)
    tn = _tile(d, 512)
    nj = d // tn
    in_specs = [pl.BlockSpec((tm, o.shape[1]), lambda i, j: (i, 0)) for o in outs]
    in_specs += [pl.BlockSpec((p.shape[0], tn), lambda i, j: (0, j)) for p in projs]
    in_specs += [pl.BlockSpec((tm, tn), functools.partial(lambda i, j, k: (i, k * nj + j), k=k)) for k in range(N_BRANCH)]
    return pl.pallas_call(
        _branch_kernel, grid=(t // tm, nj), in_specs=in_specs,
        out_specs=pl.BlockSpec((tm, tn), lambda i, j: (i, j)),
        out_shape=jax.ShapeDtypeStruct((t, d), BF16), compiler_params=_cparams(2),
    )(*outs, *projs, gates, gates, gates, gates)


def _glu_kernel(*refs, n_prefetch):
    pends_ref = refs[0] if n_prefetch else None
    x_ref, wg_ref, wu_ref, o_ref = refs[n_prefetch:n_prefetch + 4]

    def compute():
        if x_ref.dtype == U32:
            xb_sc = refs[n_prefetch + 4]

            @pl.when(pl.program_id(1) == 0)
            def _():
                lo, hi = _unpack_bf16_pairs(x_ref[...])
                xb_sc[:, :lo.shape[1]] = lo
                xb_sc[:, lo.shape[1]:] = hi
            x = xb_sc[...]
        else:
            x = x_ref[...]
        g = jnp.dot(x, wg_ref[...], preferred_element_type=F32)
        u = jnp.dot(x, wu_ref[...], preferred_element_type=F32)
        o_ref[...] = (g * jax.nn.sigmoid(g) * u).astype(o_ref.dtype)

    if n_prefetch:
        used = pl.program_id(0) * x_ref.shape[0] < pends_ref[N_EXPERTS - 1]
        pl.when(used)(compute)

        @pl.when(jnp.logical_not(used))
        def _():
            o_ref[...] = jnp.zeros(o_ref.shape, o_ref.dtype)
    else:
        compute()


def _mmk_kernel(a_ref, w_ref, o_ref, acc_sc):
    k = pl.program_id(2)
    part = jnp.dot(a_ref[...], w_ref[...], preferred_element_type=F32)

    @pl.when(k == 0)
    def _():
        acc_sc[...] = part

    @pl.when(k > 0)
    def _():
        acc_sc[...] += part

    @pl.when(k == pl.num_programs(2) - 1)
    def _():
        o_ref[...] = acc_sc[...].astype(o_ref.dtype)


def _ffn_dense(h, wg, wu, wd):
    t, d = h.shape
    ff = wg.shape[1]
    tm = _tile(t, 1024, SUBLANES)
    tf = _tile(ff, 512)
    a = pl.pallas_call(
        functools.partial(_glu_kernel, n_prefetch=0), grid=(t // tm, ff // tf),
        in_specs=[pl.BlockSpec((tm, d), lambda i, f: (i, 0)), pl.BlockSpec((d, tf), lambda i, f: (0, f)),
                  pl.BlockSpec((d, tf), lambda i, f: (0, f))],
        out_specs=pl.BlockSpec((tm, tf), lambda i, f: (i, f)),
        out_shape=jax.ShapeDtypeStruct((t, ff), BF16), compiler_params=_cparams(2))(h, wg, wu)
    tn = _tile(d, 2048)
    tk = _tile(ff, 2048)
    return pl.pallas_call(
        _mmk_kernel, grid=(t // tm, d // tn, ff // tk),
        in_specs=[pl.BlockSpec((tm, tk), lambda i, j, k: (i, k)), pl.BlockSpec((tk, tn), lambda i, j, k: (k, j))],
        out_specs=pl.BlockSpec((tm, tn), lambda i, j, k: (i, j)),
        out_shape=jax.ShapeDtypeStruct((t, d), BF16),
        scratch_shapes=[pltpu.VMEM((tm, tn), F32)], compiler_params=_cparams(3))(a, wd)


def _down_experts_kernel(pends_ref, a_ref, w_ref, o_ref):
    used = pl.program_id(0) * a_ref.shape[0] < pends_ref[N_EXPERTS - 1]

    @pl.when(used)
    def _():
        o_ref[...] = jnp.dot(a_ref[...], w_ref[...], preferred_element_type=F32)

    @pl.when(jnp.logical_not(used))
    def _():
        o_ref[...] = jnp.zeros(o_ref.shape, o_ref.dtype)


def _ffn_experts(xbuf, pends, wg, wu, wd, rows):
    p = xbuf.shape[0]
    d, ff = wg.shape[1], wg.shape[2]
    tf = _tile(ff, 512)

    def expert(i, pends_ref):
        e = jnp.int32(0)
        for j in range(N_EXPERTS - 1):
            e = e + (i * rows >= pends_ref[j]).astype(I32)
        return e

    a = pl.pallas_call(
        functools.partial(_glu_kernel, n_prefetch=1),
        grid_spec=pltpu.PrefetchScalarGridSpec(
            num_scalar_prefetch=1, grid=(p // rows, ff // tf),
            in_specs=[pl.BlockSpec((rows, d // 2), lambda i, f, pe: (i, 0)),
                      pl.BlockSpec((None, d, tf), lambda i, f, pe: (expert(i, pe), 0, f)),
                      pl.BlockSpec((None, d, tf), lambda i, f, pe: (expert(i, pe), 0, f))],
            out_specs=pl.BlockSpec((rows, tf), lambda i, f, pe: (i, f)),
            scratch_shapes=[pltpu.VMEM((rows, d), BF16)]),
        out_shape=jax.ShapeDtypeStruct((p, ff), BF16), compiler_params=_cparams(2))(pends, xbuf, wg, wu)
    tn = _tile(d, 1024)
    return pl.pallas_call(
        _down_experts_kernel,
        grid_spec=pltpu.PrefetchScalarGridSpec(
            num_scalar_prefetch=1, grid=(p // rows, d // tn),
            in_specs=[pl.BlockSpec((rows, ff), lambda i, j, pe: (i, 0)),
                      pl.BlockSpec((None, ff, tn), lambda i, j, pe: (expert(i, pe), 0, j))],
            out_specs=pl.BlockSpec((rows, tn), lambda i, j, pe: (i, j))),
        out_shape=jax.ShapeDtypeStruct((p, d), F32), compiler_params=_cparams(2))(pends, a, wd)


def _route_kernel(lg_ref, info_ref, w_ref, cnt_ref, carry_sc):
    i = pl.program_id(0)

    @pl.when(i == 0)
    def _():
        carry_sc[...] = jnp.zeros(carry_sc.shape, F32)

    tr = lg_ref.shape[0]
    lane = lax.broadcasted_iota(I32, (tr, LANES), 1)
    lg = jnp.where(lane < N_EXPERTS, lg_ref[...], -jnp.inf)
    m1 = jnp.max(lg, axis=1, keepdims=True)
    i1 = jnp.min(jnp.where(lg == m1, lane, LANES), axis=1, keepdims=True)
    lg2 = jnp.where(lane == i1, -jnp.inf, lg)
    m2 = jnp.max(lg2, axis=1, keepdims=True)
    i2 = jnp.min(jnp.where(lg2 == m2, lane, LANES), axis=1, keepdims=True)
    e = jnp.exp(m2 - m1)
    w1 = 1.0 / (1.0 + e)
    w2 = e * w1
    oh = jnp.where(jnp.logical_or(lane == i1, lane == i2), 1.0, 0.0)
    tri = jnp.where(lax.broadcasted_iota(I32, (tr, tr), 1) < lax.broadcasted_iota(I32, (tr, tr), 0), 1.0, 0.0)
    before = jnp.dot(tri.astype(BF16), oh.astype(BF16), preferred_element_type=F32) + carry_sc[0:1, :]
    r1 = jnp.sum(jnp.where(lane == i1, before, 0.0), axis=1, keepdims=True).astype(I32)
    r2 = jnp.sum(jnp.where(lane == i2, before, 0.0), axis=1, keepdims=True).astype(I32)
    carry_sc[0:1, :] = carry_sc[0:1, :] + jnp.sum(oh, axis=0, keepdims=True)
    info_ref[...] = jnp.where(lane == 0, i1, jnp.where(lane == 1, i2, jnp.where(lane == 2, r1, jnp.where(lane == 3, r2, 0))))
    w_ref[...] = jnp.where(lane == 0, w1, jnp.where(lane == 1, w2, 0.0))
    cnt_ref[...] = carry_sc[...]


def _route(logits):
    t = logits.shape[0]
    tr = _tile(t, 256, SUBLANES)
    blk = pl.BlockSpec((tr, LANES), lambda i: (i, 0))
    return pl.pallas_call(
        _route_kernel, grid=(t // tr,), in_specs=[blk],
        out_specs=[blk, blk, pl.BlockSpec((SUBLANES, LANES), lambda i: (0, 0))],
        out_shape=[jax.ShapeDtypeStruct((t, LANES), I32), jax.ShapeDtypeStruct((t, LANES), F32),
                   jax.ShapeDtypeStruct((SUBLANES, LANES), F32)],
        scratch_shapes=[pltpu.VMEM((SUBLANES, LANES), F32)], compiler_params=_cparams(1))(logits)


def _dispatch_kernel(e_ref, r_ref, ps_ref, h_ref, xz_hbm, xbuf_hbm, sem):
    del xz_hbm
    tr = h_ref.shape[0]
    base = pl.program_id(0) * tr

    def copy(r, k):
        a = (base + r) * 2 + k
        dst = ps_ref[e_ref[a]] + r_ref[a]
        return pltpu.make_async_copy(h_ref.at[pl.ds(r, 1)], xbuf_hbm.at[pl.ds(dst, 1)], sem)

    def start(r, carry):
        copy(r, 0).start(priority=0)
        copy(r, 1).start(priority=1)
        return carry
    lax.fori_loop(0, tr, start, 0)

    def wait(r, carry):
        copy(r, 0).wait()
        copy(r, 1).wait()
        return carry
    lax.fori_loop(0, tr, wait, 0)


def _dispatch(h, e_flat, r_flat, pstarts, p_rows):
    t, w = h.shape
    tr = _tile(t, 256, SUBLANES)
    grid_spec = pltpu.PrefetchScalarGridSpec(
        num_scalar_prefetch=3, grid=(t // tr,),
        in_specs=[pl.BlockSpec((tr, w), lambda i, *_: (i, 0)), pl.BlockSpec(memory_space=pl.ANY)],
        out_specs=pl.BlockSpec(memory_space=pl.ANY),
        scratch_shapes=[pltpu.SemaphoreType.DMA(())])
    return pl.pallas_call(
        _dispatch_kernel, grid_spec=grid_spec,
        out_shape=jax.ShapeDtypeStruct((p_rows, w), h.dtype),
        input_output_aliases={4: 0}, compiler_params=_cparams(1),
    )(e_flat, r_flat, pstarts, h, jnp.zeros((p_rows, w), h.dtype))


def _combine_kernel(e_ref, r_ref, ps_ref, x_ref, w_ref, mod_ref, gain_ref, y_hbm, o_ref, ybuf, sem, *, res_row):
    tr = x_ref.shape[0]
    base = pl.program_id(0) * tr

    def copy(r, k):
        a = (base + r) * 2 + k
        src = ps_ref[e_ref[a]] + r_ref[a]
        return pltpu.make_async_copy(y_hbm.at[pl.ds(src, 1)], ybuf.at[k, pl.ds(r, 1)], sem)

    def start(r, carry):
        copy(r, 0).start(priority=0)
        copy(r, 1).start(priority=1)
        return carry
    lax.fori_loop(0, tr, start, 0)

    def wait(r, carry):
        copy(r, 0).wait()
        copy(r, 1).wait()
        return carry
    lax.fori_loop(0, tr, wait, 0)

    w = w_ref[...]
    y = ybuf[0] * w[:, 0:1] + ybuf[1] * w[:, 1:2]
    x = x_ref[...] + mod_ref[res_row:res_row + 1, :] * y
    o_ref[...] = _rms(x, gain_ref[...])


def _combine_final(x, ybuf, wts, mod, gain, e_flat, r_flat, pstarts, seq, res_row):
    t, d = x.shape
    tr = _tile(seq, 256, SUBLANES)
    per_b = seq // tr
    grid_spec = pltpu.PrefetchScalarGridSpec(
        num_scalar_prefetch=3, grid=(t // tr,),
        in_specs=[pl.BlockSpec((tr, d), lambda i, *_: (i, 0)), pl.BlockSpec((tr, LANES), lambda i, *_: (i, 0)),
                  pl.BlockSpec((None, SUBLANES, d), lambda i, *_: (i // per_b, 0, 0)),
                  pl.BlockSpec((1, d), lambda i, *_: (0, 0)), pl.BlockSpec(memory_space=pl.ANY)],
        out_specs=pl.BlockSpec((tr, d), lambda i, *_: (i, 0)),
        scratch_shapes=[pltpu.VMEM((2, tr, d), F32), pltpu.SemaphoreType.DMA(())])
    return pl.pallas_call(
        functools.partial(_combine_kernel, res_row=res_row), grid_spec=grid_spec,
        out_shape=jax.ShapeDtypeStruct((t, d), F32), compiler_params=_cparams(1),
    )(e_flat, r_flat, pstarts, x, wts, mod, gain.reshape(1, d), ybuf)


def _split_w_in(w_in, d):
    sizes = (DIL_WIDTH, DIL_WIDTH, DIL_WIDTH, MLA_Q_LORA, MLA_KV_LORA, MLA_ROPE,
             MOBA_HEADS * HEAD_DIM, MOBA_HEADS * HEAD_DIM, MOBA_HEADS * HEAD_DIM,
             DSA_HEADS * HEAD_DIM, HEAD_DIM, HEAD_DIM, IDX_HEADS * IDX_DIM, IDX_DIM, IDX_HEADS, N_BRANCH * d)
    names = ("a_q", "a_k", "a_v", "b_cq", "b_ckv", "b_kr", "c_q", "c_k", "c_v",
             "d_q", "d_k", "d_v", "d_iq", "d_ik", "d_iw", "g")
    assert w_in.shape[1] == sum(sizes)
    parts, o = {}, 0
    for nme, sz in zip(names, sizes):
        parts[nme] = w_in[:, o:o + sz]
        o += sz
    return parts


def _cat_cols(parts, names, pad_to=None):
    cols, offs, o = [], {}, 0
    for nme in names:
        wpart = parts[nme]
        width = -(-wpart.shape[1] // LANES) * LANES
        if width != wpart.shape[1]:
            wpart = jnp.pad(wpart, ((0, 0), (0, width - wpart.shape[1])))
        cols.append(wpart.astype(BF16))
        offs[nme] = o // LANES
        o += width
    if pad_to is not None and o % pad_to:
        cols.append(jnp.zeros((cols[0].shape[0], pad_to - o % pad_to), BF16))
    return jnp.concatenate(cols, axis=1), offs


def _mixer(h, x, mod, bsz, seq, tables, w_in, q_norm, q_up, kv_norm, kv_up, w_branch, w_out):
    t, d = h.shape
    rope_h, rope_iq, rope_ik, rope_mla = tables
    parts = _split_w_in(w_in, d)
    w_rot, ro = _cat_cols(parts, ("c_q", "c_k", "d_q"))
    w_plain, po = _cat_cols(parts, ("b_cq", "c_v", "b_ckv", "d_v", "b_kr"), pad_to=512)
    assert ro["d_q"] * LANES % (DSA_HEADS * HEAD_DIM) == 0
    assert po["b_cq"] * LANES % MLA_Q_LORA == 0 and po["b_ckv"] * LANES % MLA_KV_LORA == 0

    rot = _matmul(h, w_rot, out_dtype=BF16, tn_pref=1024, epi="rope", rope=rope_h, half=ROT_DIM // 2)
    plain = _matmul(h, w_plain, out_dtype=BF16, tn_pref=512)
    d_k = _matmul(h, parts["d_k"].astype(BF16), out_dtype=BF16, tn_pref=LANES, epi="rope", rope=rope_h, half=ROT_DIM // 2)
    gw = DIL_OUT
    a_qk = [_matmul(h, jnp.concatenate([parts["a_q"][:, g * gw:(g + 1) * gw], parts["a_k"][:, g * gw:(g + 1) * gw]],
                                       axis=1).astype(BF16),
                    out_dtype=BF16, tn_pref=1024, epi="rope", rope=rope_h, half=ROT_DIM // 2)
            for g in range(len(DIL_GROUPS))]
    a_v = [_matmul(h, parts["a_v"][:, g * gw:(g + 1) * gw].astype(BF16), out_dtype=BF16, tn_pref=1024)
           for g in range(len(DIL_GROUPS))]
    iq = _matmul(h, parts["d_iq"].astype(BF16), out_dtype=BF16, tn_pref=1024, epi="rope", rope=rope_iq, half=IDX_ROT // 2)
    ik = _matmul(h, _cat_cols(parts, ("d_ik",))[0], out_dtype=F32, tn_pref=LANES, epi="rope", rope=rope_ik, half=IDX_ROT // 2)
    iw = _matmul(h, _cat_cols(parts, ("d_iw",))[0], out_dtype=F32, tn_pref=LANES)
    gates = _matmul(h, parts["g"].astype(BF16), out_dtype=BF16, tn_pref=1024, epi="sigmoid")

    o_a = _dilated_mixture(a_qk, a_v, bsz, seq)

    hq = MLA_NOPE + MLA_ROPE
    wq = q_up.reshape(MLA_Q_LORA, MLA_HEADS, hq)
    wq = jnp.pad(wq, ((0, 0), (0, 0), (0, MLA_QK_PAD - hq))).reshape(MLA_Q_LORA, MLA_HEADS * MLA_QK_PAD).astype(BF16)
    wkv = kv_up.reshape(MLA_KV_LORA, MLA_HEADS, MLA_NOPE + MLA_V)
    wkv = jnp.concatenate([wkv[:, :, :MLA_NOPE].reshape(MLA_KV_LORA, -1), wkv[:, :, MLA_NOPE:].reshape(MLA_KV_LORA, -1)],
                          axis=1).astype(BF16)
    q_cat, k_cat, v_b = _mla_project(plain, po["b_cq"], po["b_ckv"], po["b_kr"], q_norm, wq, kv_norm, wkv, rope_mla)
    o_b = _flash(q_cat, k_cat, v_b, bsz, seq, MLA_HEADS, MLA_QK_PAD, MLA_V, 0, 0, 0, hq ** -0.5)

    kmean = _moba_kmean(rot, bsz, seq, ro["c_k"])
    o_c = _flash(rot, rot, plain, bsz, seq, MOBA_HEADS, HEAD_DIM, HEAD_DIM, ro["c_q"], ro["c_k"], po["c_v"],
                 HEAD_DIM ** -0.5, kmean=kmean)

    o_d = _dsa_attention(rot, d_k, plain, iq, ik, iw, bsz, seq, ro["d_q"], po["d_v"])

    sizes = (DIL_OUT, MLA_HEADS * MLA_V, MOBA_HEADS * HEAD_DIM, DSA_HEADS * HEAD_DIM)
    projs, o = [], 0
    for sz in sizes:
        projs.append(w_branch[o:o + sz].astype(BF16))
        o += sz
    y = _branch_merge((o_a, o_b, o_c, o_d), projs, gates, d)
    return _matmul(y, w_out.astype(BF16), out_dtype=F32, tn_pref=512, epi="resid", xres=x, mod=mod, res_row=2, seq=seq)


def kernel(x, c, positions, w_ada, b_ada, ada_table_0, mix_norm_0, w_in_0, mla_q_norm_0, mla_q_up_0, mla_kv_norm_0, mla_kv_up_0, w_branch_0, w_out_0, ffn_norm_0, ffn_gate_0, ffn_up_0, ffn_down_0, ada_table_1, mix_norm_1, w_in_1, mla_q_norm_1, mla_q_up_1, mla_kv_norm_1, mla_kv_up_1, w_branch_1, w_out_1, ffn_norm_1, router_1, expert_gate_1, expert_up_1, expert_down_1, final_norm):
    bsz, seq, d = x.shape
    t = bsz * seq
    xf = x.reshape(t, d)
    mod0, mod1 = _ada_mod(c, w_ada, b_ada, ada_table_0, ada_table_1)

    pos_b = jnp.broadcast_to(positions.reshape(t, 1).astype(F32), (t, LANES))
    tables = (_rope_tables(pos_b, _rope_pattern(HEAD_DIM, ROT_DIM, LANES)),
              _rope_tables(pos_b, _rope_pattern(IDX_DIM, IDX_ROT, LANES)),
              _rope_tables(pos_b, _rope_pattern(IDX_DIM, IDX_ROT, IDX_DIM)),
              _rope_tables(pos_b, _rope_pattern(MLA_ROPE, MLA_ROPE, MLA_ROPE)))

    (h,) = _norm_mod(xf, mod0, mix_norm_0, seq, shift_row=0, scale_row=1)
    xf = _mixer(h, xf, mod0, bsz, seq, tables, w_in_0, mla_q_norm_0, mla_q_up_0, mla_kv_norm_0, mla_kv_up_0, w_branch_0, w_out_0)
    (h,) = _norm_mod(xf, mod0, ffn_norm_0, seq, shift_row=3, scale_row=4)
    y = _ffn_dense(h, ffn_gate_0.astype(BF16), ffn_up_0.astype(BF16), ffn_down_0.astype(BF16))

    xf, h = _norm_mod(xf, mod1, mix_norm_1, seq, shift_row=0, scale_row=1, y=y, res_mod=mod0, res_row=5)
    xf = _mixer(h, xf, mod1, bsz, seq, tables, w_in_1, mla_q_norm_1, mla_q_up_1, mla_kv_norm_1, mla_kv_up_1, w_branch_1, w_out_1)
    w_r = jnp.pad(router_1, ((0, 0), (0, LANES - N_EXPERTS))).astype(BF16)
    h, logits = _norm_mod(xf, mod1, ffn_norm_1, seq, shift_row=3, scale_row=4, w_router=w_r, h_dtype=U32)
    info, wts, cnt = _route(logits)
    e_flat = info[:, 0:2].reshape(2 * t)
    r_flat = info[:, 2:4].reshape(2 * t)
    counts = cnt[0, :N_EXPERTS].astype(I32)
    rows = min(MOE_ROWS, t)
    padded = (counts + rows - 1) // rows * rows
    pends = jnp.cumsum(padded).astype(I32)
    pstarts = pends - padded
    p_rows = (-(-2 * t // rows) + N_EXPERTS) * rows
    xbuf = _dispatch(h, e_flat, r_flat, pstarts, p_rows)
    ybuf = _ffn_experts(xbuf, pends, expert_gate_1.astype(BF16), expert_up_1.astype(BF16), expert_down_1.astype(BF16), rows)
    out = _combine_final(xf, ybuf, wts, mod1, final_norm, e_flat, r_flat, pstarts, seq, res_row=5)
    return out.reshape(bsz, seq, d)
```
